```python
import jax, jax.numpy as jnp
from jax import lax
import numpy as np

D_MODEL = 2048
BATCH = 2
SEQ = 4096
DEPTH = 4
DEC_BATCH = 32
DEC_SEQ = 64
PAST_LEN = 2048

CHUNK = 64
N_EVEN = (DEPTH + 1) // 2
N_ODD = DEPTH // 2
H_RET = 4
DK_RET = 256
DV_RET = 256
QK_RET = H_RET * DK_RET
RET_W = H_RET * DV_RET
H_FOX = 8
DH_FOX = 128
FOX_W = H_FOX * DH_FOX
MIX_W = RET_W + FOX_W
IN_SPLITS = (QK_RET, QK_RET, RET_W, RET_W, FOX_W, FOX_W, FOX_W, H_FOX)
IN_W = sum(IN_SPLITS)
POOL_WINDOWS = (2, 4, 8, 16)
N_POOL_GROUPS = len(POOL_WINDOWS)
POOL_GC = D_MODEL // N_POOL_GROUPS
POOL_BUF = max(POOL_WINDOWS) - 1
D_FF = -(-8 * D_MODEL // (3 * 256)) * 256
PLE_DIM = 256
Q_BLOCK = 128
ROPE_BASE = 10000.0
EPS = 1e-6
NEG_INF = -1e30
FORGET_BIAS = 2.0

kernel_name = 'chunk_causal_retention_fox_pool_encoder_step'


def rmsnorm(x, g):
    xf = x.astype(jnp.float32)
    y = xf * lax.rsqrt(jnp.mean(xf * xf, axis=-1, keepdims=True) + EPS)
    return (y * g.astype(jnp.float32)).astype(x.dtype)


def head_rmsnorm(x):
    xf = x.astype(jnp.float32)
    return xf * lax.rsqrt(jnp.mean(xf * xf, axis=-1, keepdims=True) + EPS)


def rope(x, pos):
    half = x.shape[-1] // 2
    inv = ROPE_BASE ** (-jnp.arange(half, dtype=jnp.float32) / half)
    ang = pos.astype(jnp.float32)[:, None] * inv[None, :]
    cos = jnp.cos(ang)[None, :, None, :]
    sin = jnp.sin(ang)[None, :, None, :]
    xf = x.astype(jnp.float32)
    x1, x2 = xf[..., :half], xf[..., half:]
    return jnp.concatenate([x1 * cos - x2 * sin, x1 * sin + x2 * cos], axis=-1).astype(x.dtype)


def retention_chunkwise(q, k, v, state):
    B, T, H, dk = q.shape
    L = min(T, CHUNK)
    n = T // L
    log_g = jnp.log1p(-jnp.power(2.0, -5.0 - jnp.arange(H, dtype=jnp.float32)))
    idx = jnp.arange(L, dtype=jnp.float32)
    intra = jnp.exp(log_g[:, None, None] * jnp.abs(idx[:, None] - idx[None, :]))
    q_dec = jnp.exp(log_g[None, :] * (idx[:, None] + 1.0))
    k_dec = jnp.exp(log_g[None, :] * (L - 1.0 - idx[:, None]))
    s_dec = jnp.exp(log_g * L)
    scale = dk ** -0.5

    def to_chunks(a):
        return jnp.moveaxis(a.astype(jnp.float32).reshape(B, n, L, H, a.shape[-1]), 1, 0)

    def step(S, qkv):
        qc, kc, vc = qkv
        kc = kc * scale
        sc = jnp.einsum('bihd,bjhd->bhij', qc, kc) * intra[None]
        o = (jnp.einsum('bhij,bjhe->bihe', sc, vc)
             + jnp.einsum('bihd,bhde->bihe', qc, S) * q_dec[None, :, :, None])
        S = S * s_dec[None, :, None, None] + jnp.einsum('bjhd,bjhe->bhde', kc * k_dec[None, :, :, None], vc)
        return S, o

    S, o = lax.scan(step, state.astype(jnp.float32), (to_chunks(q), to_chunks(k), to_chunks(v)))
    o = jnp.moveaxis(o, 0, 1).reshape(B, T, H, v.shape[-1])
    return o, S


def forgetting_attention(q, k_all, v_all, logf_all, q_off):
    B, Tq, H, d = q.shape
    Tk = k_all.shape[1]
    c_k = jnp.transpose(jnp.cumsum(logf_all.astype(jnp.float32), axis=1), (0, 2, 1))
    qb = min(Tq, Q_BLOCK)
    nb = Tq // qb
    k_pos = jnp.arange(Tk)
    scale = d ** -0.5

    def block(bi):
        start = bi * qb
        q_blk = lax.dynamic_slice_in_dim(q, start, qb, axis=1)
        q_abs = q_off + start + jnp.arange(qb)
        c_q = lax.dynamic_slice_in_dim(c_k, q_off + start, qb, axis=2)
        s = (jnp.einsum('bqhd,bkhd->bhqk', q_blk, k_all).astype(jnp.float32) * scale
             + (c_q[..., None] - c_k[:, :, None, :]))
        s = jnp.where(k_pos[None, :] <= q_abs[:, None], s, NEG_INF)
        p = jax.nn.softmax(s, axis=-1)
        return jnp.einsum('bhqk,bkhd->bqhd', p.astype(v_all.dtype), v_all)

    out = lax.map(block, jnp.arange(nb))
    return jnp.moveaxis(out, 0, 1).reshape(B, Tq, H, d)


def pool_mixer(u, buf, pos0, w_pool, pool_scale):
    B, T, D = u.shape
    full = jnp.concatenate([buf.astype(u.dtype), u], axis=1)
    cs = jnp.pad(jnp.cumsum(full.astype(jnp.float32), axis=1), ((0, 0), (1, 0), (0, 0)))
    pos = pos0 + jnp.arange(T)
    outs = []
    for g, w in enumerate(POOL_WINDOWS):
        sl = slice(g * POOL_GC, (g + 1) * POOL_GC)
        hi = cs[:, POOL_BUF + 1:POOL_BUF + 1 + T, sl]
        lo = cs[:, POOL_BUF + 1 - w:POOL_BUF + 1 - w + T, sl]
        cnt = jnp.minimum(w, pos + 1).astype(jnp.float32)
        outs.append((hi - lo) / cnt[None, :, None])
    pooled = jnp.stack(outs, axis=2)
    diff = (pooled - u.astype(jnp.float32).reshape(B, T, N_POOL_GROUPS, POOL_GC)).astype(u.dtype)
    y = jnp.einsum('btgc,gcd->btgd', diff, w_pool).reshape(B, T, D) * pool_scale
    return y, full[:, -POOL_BUF:]


def swiglu(h, w_gate, w_up, w_down):
    return (jax.nn.silu(h @ w_gate) * (h @ w_up)) @ w_down


def run_trunk(x, p, st_ret, c_k, c_v, c_lf, st_pool, pos0, norm_mix, w_in, b_forget, w_out,
              w_pool, pool_scale, norm_ffn, w_gate, w_up, w_down, norm_ple, ple_gate, ple_proj, final_norm):
    B, T, _ = x.shape
    past = c_k.shape[2]
    pos = pos0 + jnp.arange(T, dtype=jnp.int32)
    split_at = [int(s) for s in np.cumsum(IN_SPLITS)[:-1]]
    new_ret, new_k, new_v, new_lf, new_pool = [], [], [], [], []
    for i in range(DEPTH):
        h = rmsnorm(x, norm_mix[i])
        if i % 2 == 0:
            e = i // 2
            q_r, k_r, v_r, g_r, q_f, k_f, v_f, f_r = jnp.split(h @ w_in[e], split_at, axis=-1)
            q_r = rope(q_r.reshape(B, T, H_RET, DK_RET), pos)
            k_r = rope(k_r.reshape(B, T, H_RET, DK_RET), pos)
            o_r, s_new = retention_chunkwise(q_r, k_r, v_r.reshape(B, T, H_RET, DV_RET), st_ret[e])
            o_r = head_rmsnorm(o_r).astype(x.dtype).reshape(B, T, RET_W) * jax.nn.silu(g_r)
            q_f = q_f.reshape(B, T, H_FOX, DH_FOX)
            k_f = k_f.reshape(B, T, H_FOX, DH_FOX)
            v_f = v_f.reshape(B, T, H_FOX, DH_FOX)
            lf = jax.nn.log_sigmoid(f_r.astype(jnp.float32) + b_forget[e].astype(jnp.float32))
            o_f = forgetting_attention(
                q_f,
                jnp.concatenate([c_k[e].astype(k_f.dtype), k_f], axis=1),
                jnp.concatenate([c_v[e].astype(v_f.dtype), v_f], axis=1),
                jnp.concatenate([c_lf[e].astype(jnp.float32), lf], axis=1),
                past)
            mix = jnp.concatenate([o_r, o_f.reshape(B, T, FOX_W)], axis=-1) @ w_out[e]
            new_ret.append(s_new.astype(st_ret.dtype))
            new_k.append(k_f)
            new_v.append(v_f)
            new_lf.append(lf)
        else:
            o = i // 2
            mix, buf = pool_mixer(h, st_pool[o], pos0, w_pool[o], pool_scale[o])
            new_pool.append(buf)
        x = x + mix
        x = x + swiglu(rmsnorm(x, norm_ffn[i]), w_gate[i], w_up[i], w_down[i])
        gate = jax.nn.sigmoid(rmsnorm(x, norm_ple[i]) @ ple_gate[i])
        x = x + gate * (p[i] @ ple_proj[i])
    return (rmsnorm(x, final_norm), jnp.stack(new_ret), jnp.stack(new_k), jnp.stack(new_v),
            jnp.stack(new_lf), jnp.stack(new_pool))


def setup_inputs(seed: int = 0) -> dict:
    key = jax.random.key(seed)
    ks = jax.random.split(key, 26)

    def nrm(k, shape, scale):
        return jax.random.normal(k, shape, jnp.float32) * scale

    return {
        'x_prompt': nrm(ks[0], (BATCH, SEQ, D_MODEL), 1.0),
        'x_sample': nrm(ks[1], (DEC_BATCH, DEC_SEQ, D_MODEL), 1.0),
        'p_prompt': nrm(ks[2], (DEPTH, BATCH, SEQ, PLE_DIM), 1.0),
        'p_sample': nrm(ks[3], (DEPTH, DEC_BATCH, DEC_SEQ, PLE_DIM), 1.0),
        'state_ret': nrm(ks[4], (N_EVEN, DEC_BATCH, H_RET, DK_RET, DV_RET), 0.1),
        'cache_fox_k': nrm(ks[5], (N_EVEN, DEC_BATCH, PAST_LEN, H_FOX, DH_FOX), 1.0),
        'cache_fox_v': nrm(ks[6], (N_EVEN, DEC_BATCH, PAST_LEN, H_FOX, DH_FOX), 1.0),
        'cache_fox_logf': jax.nn.log_sigmoid(FORGET_BIAS + nrm(ks[7], (N_EVEN, DEC_BATCH, PAST_LEN, H_FOX), 1.0)),
        'state_pool': nrm(ks[8], (N_ODD, DEC_BATCH, POOL_BUF, D_MODEL), 1.0),
        'norm_mix': 1.0 + nrm(ks[9], (DEPTH, D_MODEL), 0.02),
        'w_in': nrm(ks[10], (N_EVEN, D_MODEL, IN_W), D_MODEL ** -0.5),
        'b_forget': FORGET_BIAS + nrm(ks[11], (N_EVEN, H_FOX), 0.1),
        'w_out': nrm(ks[12], (N_EVEN, MIX_W, D_MODEL), MIX_W ** -0.5),
        'w_pool': nrm(ks[13], (N_ODD, N_POOL_GROUPS, POOL_GC, POOL_GC), POOL_GC ** -0.5),
        'pool_scale': 1.0 + nrm(ks[14], (N_ODD, D_MODEL), 0.1),
        'norm_ffn': 1.0 + nrm(ks[15], (DEPTH, D_MODEL), 0.02),
        'w_gate': nrm(ks[16], (DEPTH, D_MODEL, D_FF), D_MODEL ** -0.5),
        'w_up': nrm(ks[17], (DEPTH, D_MODEL, D_FF), D_MODEL ** -0.5),
        'w_down': nrm(ks[18], (DEPTH, D_FF, D_MODEL), D_FF ** -0.5),
        'norm_ple': 1.0 + nrm(ks[19], (DEPTH, D_MODEL), 0.02),
        'ple_gate': nrm(ks[20], (DEPTH, D_MODEL, D_MODEL), D_MODEL ** -0.5),
        'ple_proj': nrm(ks[21], (DEPTH, PLE_DIM, D_MODEL), PLE_DIM ** -0.5),
        'final_norm': 1.0 + nrm(ks[22], (D_MODEL,), 0.02),
    }


def reference(x_prompt, x_sample, p_prompt, p_sample, state_ret, cache_fox_k, cache_fox_v, cache_fox_logf,
              state_pool, norm_mix, w_in, b_forget, w_out, w_pool, pool_scale, norm_ffn, w_gate, w_up, w_down,
              norm_ple, ple_gate, ple_proj, final_norm):
    B = x_prompt.shape[0]
    dt = x_prompt.dtype
    empty_kv = jnp.zeros((N_EVEN, B, 0, H_FOX, DH_FOX), dt)
    empty_lf = jnp.zeros((N_EVEN, B, 0, H_FOX), jnp.float32)
    zero_ret = jnp.zeros((N_EVEN, B, H_RET, DK_RET, DV_RET), jnp.float32)
    zero_pool = jnp.zeros((N_ODD, B, POOL_BUF, D_MODEL), dt)
    y_prompt, ret_p, k_p, v_p, lf_p, pool_p = run_trunk(
        x_prompt, p_prompt, zero_ret, empty_kv, empty_kv, empty_lf, zero_pool, 0,
        norm_mix, w_in, b_forget, w_out, w_pool, pool_scale, norm_ffn, w_gate, w_up, w_down,
        norm_ple, ple_gate, ple_proj, final_norm)
    y_sample, ret_s, k_s, v_s, lf_s, pool_s = run_trunk(
        x_sample, p_sample, state_ret, cache_fox_k, cache_fox_v, cache_fox_logf, state_pool, cache_fox_k.shape[2],
        norm_mix, w_in, b_forget, w_out, w_pool, pool_scale, norm_ffn, w_gate, w_up, w_down,
        norm_ple, ple_gate, ple_proj, final_norm)
    return (y_prompt, y_sample, ret_p, ret_s, k_p, k_s, v_p, v_s, lf_p, lf_s, pool_p, pool_s)
```

```python
import functools

import jax
import jax.numpy as jnp
from jax import lax
from jax.experimental import pallas as pl
from jax.experimental.pallas import tpu as pltpu

F32 = jnp.float32
BF16 = jnp.bfloat16

EPS = 1e-6
NEG_INF = -1e30
ROPE_BASE = 10000.0
CHUNK = 64
H_RET = 4
DK_RET = 256
H_FOX = 8
DH_FOX = 128
POOL_WINDOWS = (2, 4, 8, 16)
POOL_HALO = 16
LANES = 128
CUMSUM_CHUNK = 256

_NT = (((1,), (1,)), ((), ()))
_TN = (((0,), (0,)), ((), ()))


def _params(semantics, vmem_mib):
    return pltpu.CompilerParams(dimension_semantics=semantics, vmem_limit_bytes=vmem_mib << 20)


def _rms(x, g):
    return x * lax.rsqrt(jnp.mean(x * x, axis=-1, keepdims=True) + EPS) * g


def _sigmoid(x):
    return 1.0 / (1.0 + jnp.exp(-x))


def _norm_matmul_kernel(x_ref, g_ref, w_ref, o_ref, h_ref):
    @pl.when(pl.program_id(1) == 0)
    def _():
        h_ref[...] = _rms(x_ref[...], g_ref[...]).astype(h_ref.dtype)

    o_ref[...] = jnp.dot(h_ref[...], w_ref[...], preferred_element_type=F32)


def _norm_matmul_logsig_kernel(x_ref, g_ref, w_ref, b_ref, o_ref):
    h = _rms(x_ref[...], g_ref[...]).astype(BF16)
    z = jnp.dot(h, w_ref[...], preferred_element_type=F32) + b_ref[...]
    o_ref[...] = jnp.minimum(z, 0.0) - jnp.log1p(jnp.exp(-jnp.abs(z)))


def _norm_matmul(x, g, w, *, tm, tn):
    t, d = x.shape
    n = w.shape[1]
    return pl.pallas_call(
        _norm_matmul_kernel,
        out_shape=jax.ShapeDtypeStruct((t, n), F32),
        grid=(t // tm, n // tn),
        in_specs=[
            pl.BlockSpec((tm, d), lambda i, j: (i, 0)),
            pl.BlockSpec((1, d), lambda i, j: (0, 0)),
            pl.BlockSpec((d, tn), lambda i, j: (0, j)),
        ],
        out_specs=pl.BlockSpec((tm, tn), lambda i, j: (i, j)),
        scratch_shapes=[pltpu.VMEM((tm, d), BF16)],
        compiler_params=_params(("parallel", "arbitrary"), 48),
        name="norm_matmul",
    )(x, g, w)


def _forget_gate(x, g, w, b, *, tm):
    t, d = x.shape
    n = w.shape[1]
    return pl.pallas_call(
        _norm_matmul_logsig_kernel,
        out_shape=jax.ShapeDtypeStruct((t, n), F32),
        grid=(t // tm,),
        in_specs=[
            pl.BlockSpec((tm, d), lambda i: (i, 0)),
            pl.BlockSpec((1, d), lambda i: (0, 0)),
            pl.BlockSpec((d, n), lambda i: (0, 0)),
            pl.BlockSpec((1, n), lambda i: (0, 0)),
        ],
        out_specs=pl.BlockSpec((tm, n), lambda i: (i, 0)),
        compiler_params=_params(("parallel",), 32),
        name="forget_gate",
    )(x, g, w, b)


def _cumsum_kernel(lf_ref, c_ref, *, tk):
    r = lax.broadcasted_iota(jnp.int32, (CUMSUM_CHUNK, CUMSUM_CHUNK), 0)
    c = lax.broadcasted_iota(jnp.int32, (CUMSUM_CHUNK, CUMSUM_CHUNK), 1)
    upper = (r <= c).astype(F32)
    carry = jnp.zeros((H_FOX, 1), F32)
    for s in range(0, tk, CUMSUM_CHUNK):
        w = min(CUMSUM_CHUNK, tk - s)
        blk = lf_ref[0, :, s:s + w]
        cs = jnp.dot(blk, upper[:w, :w], precision=lax.Precision.HIGHEST,
                     preferred_element_type=F32) + carry
        c_ref[0, :, s:s + w] = cs
        carry = cs[:, w - 1:w]


def _cumsum_rows(lf_rows):
    b, h, tk = lf_rows.shape
    return pl.pallas_call(
        functools.partial(_cumsum_kernel, tk=tk),
        out_shape=jax.ShapeDtypeStruct((b, h, tk), F32),
        grid=(b,),
        in_specs=[pl.BlockSpec((1, h, tk), lambda i: (i, 0, 0))],
        out_specs=pl.BlockSpec((1, h, tk), lambda i: (i, 0, 0)),
        compiler_params=_params(("parallel",), 32),
        name="logf_cumsum",
    )(lf_rows)


def _rope(x, cos, sin):
    half = x.shape[-1] // 2
    x1, x2 = x[:, :half], x[:, half:]
    return jnp.concatenate([x1 * cos - x2 * sin, x1 * sin + x2 * cos], axis=-1)


def _retention_kernel(q_ref, k_ref, v_ref, g_ref, cos_ref, sin_ref, intra_ref, qdec_ref, kdec_ref,
                      sdec_ref, s0_ref, o_ref, s_out_ref, state_ref, *, chunks, scale):
    i = pl.program_id(2)

    @pl.when(i == 0)
    def _():
        state_ref[...] = s0_ref[0, 0]

    intra = intra_ref[0]
    q_dec = qdec_ref[0]
    k_dec = kdec_ref[0]
    s_dec = sdec_ref[0]
    for c in range(chunks):
        rows = slice(c * CHUNK, (c + 1) * CHUNK)
        cos = cos_ref[rows, :]
        sin = sin_ref[rows, :]
        q = _rope(q_ref[rows, :], cos, sin)
        k = _rope(k_ref[rows, :], cos, sin) * scale
        qb = q.astype(BF16)
        vb = v_ref[rows, :].astype(BF16)
        state = state_ref[...]
        sc = lax.dot_general(qb, k.astype(BF16), _NT, preferred_element_type=F32) * intra
        o = (jnp.dot(sc.astype(BF16), vb, preferred_element_type=F32)
             + jnp.dot(qb, state.astype(BF16), preferred_element_type=F32) * q_dec)
        kd = (k * k_dec).astype(BF16)
        state_ref[...] = state * s_dec + lax.dot_general(kd, vb, _TN, preferred_element_type=F32)
        o = o * lax.rsqrt(jnp.mean(o * o, axis=-1, keepdims=True) + EPS)
        gate = g_ref[rows, :]
        o_ref[rows, :] = (o * (gate * _sigmoid(gate))).astype(o_ref.dtype)

    @pl.when(i == pl.num_programs(2) - 1)
    def _():
        s_out_ref[0, 0] = state_ref[...]


def _retention(proj, cos, sin, decay, state, *, batch, seq, row_off, chunks):
    tblk = chunks * CHUNK
    nblk = seq // tblk
    off = row_off // tblk
    intra, q_dec, k_dec, s_dec = decay

    def col(base):
        return pl.BlockSpec((tblk, DK_RET), lambda b, h, i: (off + b * nblk + i, base + h))

    per_head = lambda shape: pl.BlockSpec((1,) + shape, lambda b, h, i: (h, 0, 0))
    table = pl.BlockSpec((tblk, DK_RET // 2), lambda b, h, i: (i, 0))
    state_spec = pl.BlockSpec((1, 1, DK_RET, DK_RET), lambda b, h, i: (b, h, 0, 0))
    return pl.pallas_call(
        functools.partial(_retention_kernel, chunks=chunks, scale=DK_RET ** -0.5),
        out_shape=(jax.ShapeDtypeStruct((batch * seq, H_RET * DK_RET), BF16),
                   jax.ShapeDtypeStruct((batch, H_RET, DK_RET, DK_RET), F32)),
        grid=(batch, H_RET, nblk),
        in_specs=[col(0), col(H_RET), col(2 * H_RET), col(3 * H_RET), table, table,
                  per_head((CHUNK, CHUNK)), per_head((CHUNK, 1)), per_head((CHUNK, 1)),
                  per_head((1, 1)), state_spec],
        out_specs=(pl.BlockSpec((tblk, DK_RET), lambda b, h, i: (b * nblk + i, h)), state_spec),
        scratch_shapes=[pltpu.VMEM((DK_RET, DK_RET), F32)],
        compiler_params=_params(("parallel", "parallel", "arbitrary"), 32),
        name="retention",
    )(proj, proj, proj, proj, cos, sin, intra, q_dec, k_dec, s_dec, state)


def _softmax_step(h, s, v, m_ref, l_ref, acc_ref):
    m_prev = m_ref[h]
    m_new = jnp.maximum(m_prev, jnp.max(s, axis=-1, keepdims=True))
    p = jnp.exp(s - m_new)
    alpha = jnp.exp(m_prev - m_new)
    l_ref[h] = alpha * l_ref[h] + jnp.sum(p, axis=-1, keepdims=True)
    acc_ref[h] = alpha * acc_ref[h] + jnp.dot(p.astype(BF16), v, preferred_element_type=F32)
    m_ref[h] = m_new


def _softmax_init(m_ref, l_ref, acc_ref):
    m_ref[...] = jnp.full(m_ref.shape, NEG_INF, F32)
    l_ref[...] = jnp.zeros(l_ref.shape, F32)
    acc_ref[...] = jnp.zeros(acc_ref.shape, F32)


def _softmax_finish(o_ref, l_ref, acc_ref):
    for h in range(H_FOX):
        o_ref[:, h * DH_FOX:(h + 1) * DH_FOX] = (acc_ref[h] / l_ref[h]).astype(o_ref.dtype)


def _fox_prompt_kernel(q_ref, k_ref, v_ref, cq_ref, ck_ref, o_ref, m_ref, l_ref, acc_ref, *, tq, scale):
    qi = pl.program_id(1)
    kj = pl.program_id(2)

    @pl.when(kj == 0)
    def _():
        _softmax_init(m_ref, l_ref, acc_ref)

    @pl.when(kj <= qi)
    def _():
        row = qi * tq + lax.broadcasted_iota(jnp.int32, (tq, tq), 0)
        col = kj * tq + lax.broadcasted_iota(jnp.int32, (tq, tq), 1)
        visible = col <= row
        for h in range(H_FOX):
            cols = slice(h * DH_FOX, (h + 1) * DH_FOX)
            q = q_ref[:, cols].astype(BF16)
            k = k_ref[:, cols].astype(BF16)
            s = (lax.dot_general(q, k, _NT, preferred_element_type=F32) * scale
                 + (cq_ref[0, :, h:h + 1] - ck_ref[0, h:h + 1, :]))
            s = jnp.where(visible, s, NEG_INF)
            _softmax_step(h, s, v_ref[:, cols].astype(BF16), m_ref, l_ref, acc_ref)

    @pl.when(kj == qi)
    def _():
        _softmax_finish(o_ref, l_ref, acc_ref)


def _fox_prompt(proj, c_col, c_row, *, batch, seq, row_off, tq, col_base):
    nq = seq // tq
    off = row_off // tq
    width = H_FOX * DH_FOX
    qb, kb, vb = (col_base + n for n in range(3))

    return pl.pallas_call(
        functools.partial(_fox_prompt_kernel, tq=tq, scale=DH_FOX ** -0.5),
        out_shape=jax.ShapeDtypeStruct((batch * seq, width), BF16),
        grid=(batch, nq, nq),
        in_specs=[
            pl.BlockSpec((tq, width), lambda b, i, j: (off + b * nq + i, qb)),
            pl.BlockSpec((tq, width), lambda b, i, j: (off + b * nq + jnp.minimum(i, j), kb)),
            pl.BlockSpec((tq, width), lambda b, i, j: (off + b * nq + jnp.minimum(i, j), vb)),
            pl.BlockSpec((1, tq, H_FOX), lambda b, i, j: (b, i, 0)),
            pl.BlockSpec((1, H_FOX, tq), lambda b, i, j: (b, 0, jnp.minimum(i, j))),
        ],
        out_specs=pl.BlockSpec((tq, width), lambda b, i, j: (b * nq + i, 0)),
        scratch_shapes=[pltpu.VMEM((H_FOX, tq, 1), F32), pltpu.VMEM((H_FOX, tq, 1), F32),
                        pltpu.VMEM((H_FOX, tq, DH_FOX), F32)],
        compiler_params=_params(("parallel", "parallel", "arbitrary"), 48),
        name="fox_prompt",
    )(proj, proj, proj, c_col, c_row)


def _fox_sample_kernel(q_ref, kc_ref, vc_ref, kn_ref, vn_ref, cq_ref, ckc_ref, ckn_ref, o_ref,
                       m_ref, l_ref, acc_ref, *, tq, scale):
    j = pl.program_id(1)
    last = pl.num_programs(1) - 1

    @pl.when(j == 0)
    def _():
        _softmax_init(m_ref, l_ref, acc_ref)

    @pl.when(j < last)
    def _():
        for h in range(H_FOX):
            cols = slice(h * DH_FOX, (h + 1) * DH_FOX)
            q = q_ref[:, cols].astype(BF16)
            k = kc_ref[0, :, cols].astype(BF16)
            s = (lax.dot_general(q, k, _NT, preferred_element_type=F32) * scale
                 + (cq_ref[0, :, h:h + 1] - ckc_ref[0, h:h + 1, :]))
            _softmax_step(h, s, vc_ref[0, :, cols].astype(BF16), m_ref, l_ref, acc_ref)

    @pl.when(j == last)
    def _():
        row = lax.broadcasted_iota(jnp.int32, (tq, tq), 0)
        col = lax.broadcasted_iota(jnp.int32, (tq, tq), 1)
        visible = col <= row
        for h in range(H_FOX):
            cols = slice(h * DH_FOX, (h + 1) * DH_FOX)
            q = q_ref[:, cols].astype(BF16)
            k = kn_ref[:, cols].astype(BF16)
            s = (lax.dot_general(q, k, _NT, preferred_element_type=F32) * scale
                 + (cq_ref[0, :, h:h + 1] - ckn_ref[0, h:h + 1, :]))
            s = jnp.where(visible, s, NEG_INF)
            _softmax_step(h, s, vn_ref[:, cols].astype(BF16), m_ref, l_ref, acc_ref)
        _softmax_finish(o_ref, l_ref, acc_ref)


def _fox_sample(proj, cache_k, cache_v, c_col, c_row_cache, c_row_new, *, batch, seq, row_off, tkc,
                col_base):
    past = cache_k.shape[1]
    nkc = past // tkc
    off = row_off // seq
    width = H_FOX * DH_FOX
    qb, kb, vb = (col_base + n for n in range(3))
    cached = lambda j: jnp.minimum(j, nkc - 1)

    return pl.pallas_call(
        functools.partial(_fox_sample_kernel, tq=seq, scale=DH_FOX ** -0.5),
        out_shape=jax.ShapeDtypeStruct((batch * seq, width), BF16),
        grid=(batch, nkc + 1),
        in_specs=[
            pl.BlockSpec((seq, width), lambda b, j: (off + b, qb)),
            pl.BlockSpec((1, tkc, width), lambda b, j: (b, cached(j), 0)),
            pl.BlockSpec((1, tkc, width), lambda b, j: (b, cached(j), 0)),
            pl.BlockSpec((seq, width), lambda b, j: (off + b, kb)),
            pl.BlockSpec((seq, width), lambda b, j: (off + b, vb)),
            pl.BlockSpec((1, seq, H_FOX), lambda b, j: (b, 0, 0)),
            pl.BlockSpec((1, H_FOX, tkc), lambda b, j: (b, 0, cached(j))),
            pl.BlockSpec((1, H_FOX, seq), lambda b, j: (b, 0, 0)),
        ],
        out_specs=pl.BlockSpec((seq, width), lambda b, j: (b, 0)),
        scratch_shapes=[pltpu.VMEM((H_FOX, seq, 1), F32), pltpu.VMEM((H_FOX, seq, 1), F32),
                        pltpu.VMEM((H_FOX, seq, DH_FOX), F32)],
        compiler_params=_params(("parallel", "arbitrary"), 48),
        name="fox_sample",
    )(proj, cache_k, cache_v, proj, proj, c_col, c_row_cache, c_row_new)


def _out_proj_kernel(a_ref, b_ref, w_ref, x_ref, o_ref):
    ka = a_ref.shape[1]
    o_ref[...] = (x_ref[...]
                  + jnp.dot(a_ref[...], w_ref[:ka, :], preferred_element_type=F32)
                  + jnp.dot(b_ref[...], w_ref[ka:, :], preferred_element_type=F32))


def _out_proj(a, b, w, x, *, tm, tn):
    t, d = x.shape
    ka, kb = a.shape[1], b.shape[1]
    return pl.pallas_call(
        _out_proj_kernel,
        out_shape=jax.ShapeDtypeStruct((t, d), F32),
        grid=(t // tm, d // tn),
        in_specs=[
            pl.BlockSpec((tm, ka), lambda i, j: (i, 0)),
            pl.BlockSpec((tm, kb), lambda i, j: (i, 0)),
            pl.BlockSpec((ka + kb, tn), lambda i, j: (0, j)),
            pl.BlockSpec((tm, tn), lambda i, j: (i, j)),
        ],
        out_specs=pl.BlockSpec((tm, tn), lambda i, j: (i, j)),
        compiler_params=_params(("parallel", "arbitrary"), 48),
        name="out_proj",
    )(a, b, w, x)


def _pool_kernel(x_ref, halo_ref, buf_ref, g_ref, w_ref, ps_ref, o_ref, bo_ref, full_ref, *, tm, pos0):
    i = pl.program_id(1)
    g = g_ref[...]
    x = x_ref[...]
    u = _rms(x, g)
    hist = jnp.where(i == 0, buf_ref[0], _rms(halo_ref[...], g))
    full_ref[0:POOL_HALO, :] = hist
    full_ref[POOL_HALO:POOL_HALO + tm, :] = u
    pos = pos0 + i * tm + lax.broadcasted_iota(jnp.int32, (tm, 1), 0)
    gc = x.shape[1] // len(POOL_WINDOWS)
    for n, win in enumerate(POOL_WINDOWS):
        cols = slice(n * gc, (n + 1) * gc)
        total = u[:, cols]
        for back in range(1, win):
            total = total + full_ref[POOL_HALO - back:POOL_HALO - back + tm, cols]
        inv_cnt = 1.0 / jnp.minimum(win, pos + 1).astype(F32)
        diff = (total * inv_cnt - u[:, cols]).astype(BF16)
        y = jnp.dot(diff, w_ref[n], preferred_element_type=F32) * ps_ref[:, cols]
        o_ref[:, cols] = x[:, cols] + y
    bo_ref[0] = u[tm - POOL_HALO:, :]


def _pool_mixer(x, buf, g, w, ps, *, batch, seq, row_off, tm, pos0):
    d = x.shape[1]
    nt = seq // tm
    off = row_off // tm
    per_halo = tm // POOL_HALO
    halo_off = row_off // POOL_HALO

    def halo_map(b, i):
        return (jnp.maximum(halo_off + (b * nt + i) * per_halo - 1, 0), 0)

    return pl.pallas_call(
        functools.partial(_pool_kernel, tm=tm, pos0=pos0),
        out_shape=(jax.ShapeDtypeStruct((batch * seq, d), F32),
                   jax.ShapeDtypeStruct((batch, POOL_HALO, d), F32)),
        grid=(batch, nt),
        in_specs=[
            pl.BlockSpec((tm, d), lambda b, i: (off + b * nt + i, 0)),
            pl.BlockSpec((POOL_HALO, d), halo_map),
            pl.BlockSpec((1, POOL_HALO, d), lambda b, i: (b, 0, 0)),
            pl.BlockSpec((1, d), lambda b, i: (0, 0)),
            pl.BlockSpec(w.shape, lambda b, i: (0, 0, 0)),
            pl.BlockSpec((1, d), lambda b, i: (0, 0)),
        ],
        out_specs=(pl.BlockSpec((tm, d), lambda b, i: (b * nt + i, 0)),
                   pl.BlockSpec((1, POOL_HALO, d), lambda b, i: (b, 0, 0))),
        scratch_shapes=[pltpu.VMEM((POOL_HALO + tm, d), F32)],
        compiler_params=_params(("parallel", "arbitrary"), 48),
        name="pool_mixer",
    )(x, x, buf, g, w, ps)


def _ffn_kernel(x_ref, g_ref, wg_ref, wu_ref, wd_ref, o_ref, h_ref):
    @pl.when(pl.program_id(1) == 0)
    def _():
        x = x_ref[...]
        h_ref[...] = _rms(x, g_ref[...]).astype(h_ref.dtype)
        o_ref[...] = x

    h = h_ref[...]
    a = jnp.dot(h, wg_ref[...], preferred_element_type=F32)
    u = jnp.dot(h, wu_ref[...], preferred_element_type=F32)
    act = (a * _sigmoid(a) * u).astype(BF16)
    o_ref[...] += jnp.dot(act, wd_ref[...], preferred_element_type=F32)


def _ffn(x, g, wg, wu, wd, *, tm, tf):
    t, d = x.shape
    f = wg.shape[1]
    return pl.pallas_call(
        _ffn_kernel,
        out_shape=jax.ShapeDtypeStruct((t, d), F32),
        grid=(t // tm, f // tf),
        in_specs=[
            pl.BlockSpec((tm, d), lambda i, j: (i, 0)),
            pl.BlockSpec((1, d), lambda i, j: (0, 0)),
            pl.BlockSpec((d, tf), lambda i, j: (0, j)),
            pl.BlockSpec((d, tf), lambda i, j: (0, j)),
            pl.BlockSpec((tf, d), lambda i, j: (j, 0)),
        ],
        out_specs=pl.BlockSpec((tm, d), lambda i, j: (i, 0)),
        scratch_shapes=[pltpu.VMEM((tm, d), BF16)],
        compiler_params=_params(("parallel", "arbitrary"), 48),
        name="swiglu_ffn",
    )(x, g, wg, wu, wd)


def _ple_kernel(x_ref, xc_ref, g_ref, wg_ref, p_ref, wp_ref, o_ref, h_ref):
    @pl.when(pl.program_id(1) == 0)
    def _():
        h_ref[...] = _rms(x_ref[...], g_ref[...]).astype(h_ref.dtype)

    gate = _sigmoid(jnp.dot(h_ref[...], wg_ref[...], preferred_element_type=F32))
    emb = jnp.dot(p_ref[...].astype(BF16), wp_ref[...], preferred_element_type=F32)
    o_ref[...] = xc_ref[...] + gate * emb


def _ple(x, g, wg, p, wp, *, tm, tn):
    t, d = x.shape
    pd = p.shape[1]
    return pl.pallas_call(
        _ple_kernel,
        out_shape=jax.ShapeDtypeStruct((t, d), F32),
        grid=(t // tm, d // tn),
        in_specs=[
            pl.BlockSpec((tm, d), lambda i, j: (i, 0)),
            pl.BlockSpec((tm, tn), lambda i, j: (i, j)),
            pl.BlockSpec((1, d), lambda i, j: (0, 0)),
            pl.BlockSpec((d, tn), lambda i, j: (0, j)),
            pl.BlockSpec((tm, pd), lambda i, j: (i, 0)),
            pl.BlockSpec((pd, tn), lambda i, j: (0, j)),
        ],
        out_specs=pl.BlockSpec((tm, tn), lambda i, j: (i, j)),
        scratch_shapes=[pltpu.VMEM((tm, d), BF16)],
        compiler_params=_params(("parallel", "arbitrary"), 48),
        name="gated_embedding",
    )(x, x, g, wg, p, wp)


def _final_norm_kernel(x_ref, g_ref, o_ref):
    o_ref[...] = _rms(x_ref[...], g_ref[...])


def _final_norm(x, g, *, tm):
    t, d = x.shape
    return pl.pallas_call(
        _final_norm_kernel,
        out_shape=jax.ShapeDtypeStruct((t, d), F32),
        grid=(t // tm,),
        in_specs=[pl.BlockSpec((tm, d), lambda i: (i, 0)), pl.BlockSpec((1, d), lambda i: (0, 0))],
        out_specs=pl.BlockSpec((tm, d), lambda i: (i, 0)),
        compiler_params=_params(("parallel",), 32),
        name="final_norm",
    )(x, g)


def _rope_tables(pos0, seq):
    half = DK_RET // 2
    inv = ROPE_BASE ** (-jnp.arange(half, dtype=F32) / half)
    ang = (pos0 + jnp.arange(seq, dtype=jnp.int32)).astype(F32)[:, None] * inv[None, :]
    return jnp.cos(ang), jnp.sin(ang)


def _decay_tables():
    log_g = jnp.log1p(-jnp.power(2.0, -5.0 - jnp.arange(H_RET, dtype=F32)))
    idx = jnp.arange(CHUNK, dtype=F32)
    intra = jnp.exp(log_g[:, None, None] * jnp.abs(idx[:, None] - idx[None, :]))
    q_dec = jnp.exp(log_g[:, None] * (idx[None, :] + 1.0))[..., None]
    k_dec = jnp.exp(log_g[:, None] * (CHUNK - 1.0 - idx[None, :]))[..., None]
    s_dec = jnp.exp(log_g * CHUNK)[:, None, None]
    return intra, q_dec, k_dec, s_dec


def _tile(n, want):
    t = min(n, want)
    while n % t:
        t //= 2
    return t


def kernel(x_prompt, x_sample, p_prompt, p_sample, state_ret, cache_fox_k, cache_fox_v, cache_fox_logf,
           state_pool, norm_mix, w_in, b_forget, w_out, w_pool, pool_scale, norm_ffn, w_gate, w_up, w_down,
           norm_ple, ple_gate, ple_proj, final_norm):
    bp, tp, d = x_prompt.shape
    bs, ts, _ = x_sample.shape
    depth = norm_mix.shape[0]
    past = cache_fox_k.shape[2]
    n_p, n_s = bp * tp, bs * ts
    ret_w = H_RET * DK_RET
    fox_w = H_FOX * DH_FOX
    main_w = 4 * ret_w + 3 * fox_w
    fox_base = (4 * ret_w) // fox_w

    x = jnp.concatenate([x_prompt.reshape(n_p, d), x_sample.reshape(n_s, d)], axis=0)
    n_tok = n_p + n_s
    tm = _tile(n_tok, 512)

    decay = _decay_tables()
    rope_p = _rope_tables(0, tp)
    rope_s = _rope_tables(past, ts)
    zero_state = jnp.zeros((bp,) + state_ret.shape[2:], F32)
    zero_buf = jnp.zeros((bp, POOL_HALO, d), F32)

    new_ret_p, new_ret_s, new_k, new_v, new_lf, new_pool_p, new_pool_s = [], [], [], [], [], [], []
    for i in range(depth):
        g_mix = norm_mix[i][None, :]
        if i % 2 == 0:
            e = i // 2
            w_main = w_in[e][:, :main_w].astype(BF16)
            w_f = jnp.pad(w_in[e][:, main_w:], ((0, 0), (0, LANES - H_FOX))).astype(BF16)
            b_f = jnp.pad(b_forget[e], (0, LANES - H_FOX))[None, :]
            proj = _norm_matmul(x, g_mix, w_main, tm=tm, tn=512)
            lf = _forget_gate(x, g_mix, w_f, b_f, tm=tm)[:, :H_FOX]
            lf_p = lf[:n_p].reshape(bp, tp, H_FOX)
            lf_s = lf[n_p:].reshape(bs, ts, H_FOX)
            c_row_p = _cumsum_rows(jnp.transpose(lf_p, (0, 2, 1)))
            lf_all_s = jnp.concatenate([cache_fox_logf[e].astype(F32), lf_s], axis=1)
            c_row_s = _cumsum_rows(jnp.transpose(lf_all_s, (0, 2, 1)))
            c_col_p = jnp.transpose(c_row_p, (0, 2, 1))
            c_col_s = jnp.transpose(c_row_s[:, :, past:], (0, 2, 1))

            o_r_p, s_p = _retention(proj, *rope_p, decay, zero_state, batch=bp, seq=tp, row_off=0,
                                    chunks=_tile(tp, 512) // CHUNK)
            o_r_s, s_s = _retention(proj, *rope_s, decay, state_ret[e].astype(F32), batch=bs, seq=ts,
                                    row_off=n_p, chunks=ts // CHUNK)
            o_f_p = _fox_prompt(proj, c_col_p, c_row_p, batch=bp, seq=tp, row_off=0, tq=_tile(tp, 512),
                                col_base=fox_base)
            o_f_s = _fox_sample(proj, cache_fox_k[e].reshape(bs, past, fox_w),
                                cache_fox_v[e].reshape(bs, past, fox_w), c_col_s, c_row_s,
                                c_row_s[:, :, past:], batch=bs, seq=ts, row_off=n_p,
                                tkc=_tile(past, 1024), col_base=fox_base)
            o_r = jnp.concatenate([o_r_p, o_r_s], axis=0)
            o_f = jnp.concatenate([o_f_p, o_f_s], axis=0)
            x = _out_proj(o_r, o_f, w_out[e].astype(BF16), x, tm=tm, tn=_tile(d, 1024))

            k_f = proj[:, 4 * ret_w + fox_w:4 * ret_w + 2 * fox_w]
            v_f = proj[:, 4 * ret_w + 2 * fox_w:main_w]
            new_ret_p.append(s_p)
            new_ret_s.append(s_s.astype(state_ret.dtype))
            new_k.append(k_f)
            new_v.append(v_f)
            new_lf.append((lf_p, lf_s))
        else:
            o = i // 2
            wp = w_pool[o].astype(BF16)
            ps = pool_scale[o][None, :]
            buf_s = jnp.pad(state_pool[o].astype(F32), ((0, 0), (1, 0), (0, 0)))
            x_p, pool_p = _pool_mixer(x, zero_buf, g_mix, wp, ps, batch=bp, seq=tp, row_off=0,
                                      tm=_tile(tp, 512), pos0=0)
            x_s, pool_s = _pool_mixer(x, buf_s, g_mix, wp, ps, batch=bs, seq=ts, row_off=n_p,
                                      tm=ts, pos0=past)
            x = jnp.concatenate([x_p, x_s], axis=0)
            new_pool_p.append(pool_p[:, 1:])
            new_pool_s.append(pool_s[:, 1:])
        x = _ffn(x, norm_ffn[i][None, :], w_gate[i].astype(BF16), w_up[i].astype(BF16),
                 w_down[i].astype(BF16), tm=tm, tf=512)
        p = jnp.concatenate([p_prompt[i].reshape(n_p, -1), p_sample[i].reshape(n_s, -1)], axis=0)
        x = _ple(x, norm_ple[i][None, :], ple_gate[i].astype(BF16), p, ple_proj[i].astype(BF16),
                 tm=tm, tn=_tile(d, 1024))

    y = _final_norm(x, final_norm[None, :], tm=tm)
    heads = (H_FOX, DH_FOX)
    return (
        y[:n_p].reshape(bp, tp, d),
        y[n_p:].reshape(bs, ts, d),
        jnp.stack(new_ret_p),
        jnp.stack(new_ret_s),
        jnp.stack([k[:n_p].reshape((bp, tp) + heads) for k in new_k]),
        jnp.stack([k[n_p:].reshape((bs, ts) + heads) for k in new_k]),
        jnp.stack([v[:n_p].reshape((bp, tp) + heads) for v in new_v]),
        jnp.stack([v[n_p:].reshape((bs, ts) + heads) for v in new_v]),
        jnp.stack([lf[0] for lf in new_lf]),
        jnp.stack([lf[1] for lf in new_lf]),
        jnp.stack(new_pool_p),
        jnp.stack(new_pool_s),
    )
```

```python
import functools
import math

import jax
import jax.numpy as jnp
from jax import lax
from jax.experimental import pallas as pl
from jax.experimental.pallas import tpu as pltpu

F32 = jnp.float32
BF16 = jnp.bfloat16

EPS = 1e-6
NEG_INF = -1e30
ROPE_BASE = 10000.0
LOG2E = math.log2(math.e)
CHUNK = 64
H_RET = 4
DK_RET = 256
H_FOX = 8
DH_FOX = 128
POOL_WINDOWS = (2, 4, 8, 16)
POOL_HALO = 16
LANES = 128
CUMSUM_CHUNK = 256

_NT = (((1,), (1,)), ((), ()))
_TN = (((0,), (0,)), ((), ()))


def _params(semantics, vmem_mib):
    return pltpu.CompilerParams(dimension_semantics=semantics, vmem_limit_bytes=vmem_mib << 20)


def _rms(x, g):
    return x * lax.rsqrt(jnp.mean(x * x, axis=-1, keepdims=True) + EPS) * g


def _sigmoid(x):
    return 1.0 / (1.0 + jnp.exp(-x))


def _layer_vec(layer, d):
    return pl.BlockSpec((None, 1, d), lambda *_: (layer, 0, 0))


def _call_into(kernel, dest, n_in, **kwargs):
    if dest is None:
        return pl.pallas_call(kernel, **kwargs)

    def body(*refs):
        kernel(*refs[:n_in], *refs[n_in + 1:])

    kwargs["in_specs"] = list(kwargs["in_specs"]) + [pl.BlockSpec(memory_space=pl.ANY)]
    call = pl.pallas_call(body, input_output_aliases={n_in: 0}, **kwargs)
    return lambda *args: call(*args, dest)


def _norm_matmul_kernel(x_ref, g_ref, w_ref, o_ref, h_ref):
    @pl.when(pl.program_id(1) == 0)
    def _():
        h_ref[...] = _rms(x_ref[...], g_ref[...]).astype(h_ref.dtype)

    o_ref[...] = jnp.dot(h_ref[...], w_ref[...], preferred_element_type=F32)


def _norm_matmul_logsig_kernel(x_ref, g_ref, w_ref, b_ref, o_ref):
    h = _rms(x_ref[...], g_ref[...]).astype(BF16)
    z = jnp.dot(h, w_ref[...], preferred_element_type=F32) + b_ref[...]
    o_ref[...] = jnp.minimum(z, 0.0) - jnp.log1p(jnp.exp(-jnp.abs(z)))


def _norm_matmul(x, g, w, *, layer, wlayer, n, tm, tn):
    t, d = x.shape
    return pl.pallas_call(
        _norm_matmul_kernel,
        out_shape=jax.ShapeDtypeStruct((t, n), F32),
        grid=(t // tm, n // tn),
        in_specs=[
            pl.BlockSpec((tm, d), lambda i, j: (i, 0)),
            _layer_vec(layer, d),
            pl.BlockSpec((None, d, tn), lambda i, j: (wlayer, 0, j)),
        ],
        out_specs=pl.BlockSpec((tm, tn), lambda i, j: (i, j)),
        scratch_shapes=[pltpu.VMEM((tm, d), BF16)],
        compiler_params=_params(("parallel", "arbitrary"), 48),
        name="norm_matmul",
    )(x, g, w)


def _forget_gate(x, g, w, b, *, layer, wlayer, tm):
    t, d = x.shape
    n = w.shape[2]
    return pl.pallas_call(
        _norm_matmul_logsig_kernel,
        out_shape=jax.ShapeDtypeStruct((t, n), F32),
        grid=(t // tm,),
        in_specs=[
            pl.BlockSpec((tm, d), lambda i: (i, 0)),
            _layer_vec(layer, d),
            pl.BlockSpec((None, d, n), lambda i: (wlayer, 0, 0)),
            pl.BlockSpec((None, 1, n), lambda i: (wlayer, 0, 0)),
        ],
        out_specs=pl.BlockSpec((tm, n), lambda i: (i, 0)),
        compiler_params=_params(("parallel",), 32),
        name="forget_gate",
    )(x, g, w, b)


def _cumsum_kernel(lf_ref, c_ref, *, tk):
    r = lax.broadcasted_iota(jnp.int32, (CUMSUM_CHUNK, CUMSUM_CHUNK), 0)
    c = lax.broadcasted_iota(jnp.int32, (CUMSUM_CHUNK, CUMSUM_CHUNK), 1)
    upper = (r <= c).astype(F32)
    carry = jnp.zeros((H_FOX, 1), F32)
    for s in range(0, tk, CUMSUM_CHUNK):
        w = min(CUMSUM_CHUNK, tk - s)
        blk = lf_ref[0, :, s:s + w]
        cs = jnp.dot(blk, upper[:w, :w], precision=lax.Precision.HIGHEST,
                     preferred_element_type=F32) + carry
        c_ref[0, :, s:s + w] = cs
        carry = cs[:, w - 1:w]


def _cumsum_rows(lf_rows):
    b, h, tk = lf_rows.shape
    return pl.pallas_call(
        functools.partial(_cumsum_kernel, tk=tk),
        out_shape=jax.ShapeDtypeStruct((b, h, tk), F32),
        grid=(b,),
        in_specs=[pl.BlockSpec((1, h, tk), lambda i: (i, 0, 0))],
        out_specs=pl.BlockSpec((1, h, tk), lambda i: (i, 0, 0)),
        compiler_params=_params(("parallel",), 32),
        name="logf_cumsum",
    )(lf_rows)


def _rope(x, cos, sin):
    half = x.shape[-1] // 2
    x1, x2 = x[:, :half], x[:, half:]
    return jnp.concatenate([x1 * cos - x2 * sin, x1 * sin + x2 * cos], axis=-1)


def _retention_kernel(q_ref, k_ref, v_ref, g_ref, cos_ref, sin_ref, intra_ref, qdec_ref, kdec_ref,
                      sdec_ref, s0_ref, o_ref, s_out_ref, state_ref, *, chunks, scale):
    i = pl.program_id(2)

    @pl.when(i == 0)
    def _():
        state_ref[...] = s0_ref[0, 0]

    intra = intra_ref[0]
    q_dec = qdec_ref[0]
    k_dec = kdec_ref[0]
    s_dec = sdec_ref[0]
    for c in range(chunks):
        rows = slice(c * CHUNK, (c + 1) * CHUNK)
        cos = cos_ref[rows, :]
        sin = sin_ref[rows, :]
        q = _rope(q_ref[rows, :], cos, sin)
        k = _rope(k_ref[rows, :], cos, sin) * scale
        qb = q.astype(BF16)
        vb = v_ref[rows, :].astype(BF16)
        state = state_ref[...]
        sc = lax.dot_general(qb, k.astype(BF16), _NT, preferred_element_type=F32) * intra
        o = (jnp.dot(sc.astype(BF16), vb, preferred_element_type=F32)
             + jnp.dot(qb, state.astype(BF16), preferred_element_type=F32) * q_dec)
        kd = (k * k_dec).astype(BF16)
        state_ref[...] = state * s_dec + lax.dot_general(kd, vb, _TN, preferred_element_type=F32)
        o = o * lax.rsqrt(jnp.mean(o * o, axis=-1, keepdims=True) + EPS)
        gate = g_ref[rows, :]
        o_ref[rows, :] = (o * (gate * _sigmoid(gate))).astype(o_ref.dtype)

    @pl.when(i == pl.num_programs(2) - 1)
    def _():
        s_out_ref[0, 0] = state_ref[...]


def _retention(proj, cos, sin, decay, state, dest, *, layer, batch, seq, row_off, chunks):
    tblk = chunks * CHUNK
    nblk = seq // tblk
    off = row_off // tblk
    intra, q_dec, k_dec, s_dec = decay

    def col(base):
        return pl.BlockSpec((tblk, DK_RET), lambda b, h, i: (off + b * nblk + i, base + h))

    per_head = lambda shape: pl.BlockSpec((1,) + shape, lambda b, h, i: (h, 0, 0))
    table = pl.BlockSpec((tblk, DK_RET // 2), lambda b, h, i: (i, 0))
    call = _call_into(
        functools.partial(_retention_kernel, chunks=chunks, scale=DK_RET ** -0.5), dest, 11,
        out_shape=(jax.ShapeDtypeStruct((proj.shape[0], H_RET * DK_RET), BF16),
                   jax.ShapeDtypeStruct((batch, H_RET, DK_RET, DK_RET), F32)),
        grid=(batch, H_RET, nblk),
        in_specs=[col(0), col(H_RET), col(2 * H_RET), col(3 * H_RET), table, table,
                  per_head((CHUNK, CHUNK)), per_head((CHUNK, 1)), per_head((CHUNK, 1)),
                  per_head((1, 1)),
                  pl.BlockSpec((None, 1, 1, DK_RET, DK_RET), lambda b, h, i: (layer, b, h, 0, 0))],
        out_specs=(col(0),
                   pl.BlockSpec((1, 1, DK_RET, DK_RET), lambda b, h, i: (b, h, 0, 0))),
        scratch_shapes=[pltpu.VMEM((DK_RET, DK_RET), F32)],
        compiler_params=_params(("parallel", "parallel", "arbitrary"), 32),
        name="retention",
    )
    return call(proj, proj, proj, proj, cos, sin, intra, q_dec, k_dec, s_dec, state)


def _fox_prompt_kernel(q_ref, k_ref, v_ref, cq_ref, ck_ref, o_ref, m_ref, l_ref, acc_ref, *, tq, scale):
    qi = pl.program_id(1)
    kj = pl.program_id(2)

    @pl.when(kj == 0)
    def _():
        m_ref[...] = jnp.full(m_ref.shape, NEG_INF, F32)
        l_ref[...] = jnp.zeros(l_ref.shape, F32)
        acc_ref[...] = jnp.zeros(acc_ref.shape, F32)

    def step(masked):
        if masked:
            key = lax.broadcasted_iota(jnp.int32, (tq, tq), 0)
            qry = lax.broadcasted_iota(jnp.int32, (tq, tq), 1)
            visible = key <= qry
        for h in range(H_FOX):
            cols = slice(h * DH_FOX, (h + 1) * DH_FOX)
            q = (q_ref[:, cols] * (scale * LOG2E)).astype(BF16)
            k = k_ref[:, cols].astype(BF16)
            bias = cq_ref[0, h:h + 1, :] * LOG2E - ck_ref[0, :, h:h + 1] * LOG2E
            s = lax.dot_general(k, q, _NT, preferred_element_type=F32) + bias
            if masked:
                s = jnp.where(visible, s, NEG_INF)
            m_prev = m_ref[h]
            m_new = jnp.maximum(m_prev, jnp.max(s, axis=0, keepdims=True))
            p = jnp.exp2(s - m_new)
            alpha = jnp.exp2(m_prev - m_new)
            l_ref[h] = alpha * l_ref[h] + jnp.sum(p, axis=0, keepdims=True)
            pv = lax.dot_general(v_ref[:, cols].astype(BF16), p.astype(BF16), _TN,
                                 preferred_element_type=F32)
            acc_ref[h] = alpha * acc_ref[h] + pv
            m_ref[h] = m_new

    @pl.when(kj < qi)
    def _():
        step(False)

    @pl.when(kj == qi)
    def _():
        step(True)
        for h in range(H_FOX):
            o = (acc_ref[h] / l_ref[h]).T
            o_ref[:, h * DH_FOX:(h + 1) * DH_FOX] = o.astype(o_ref.dtype)


def _fox_prompt(proj, c_col, c_row, *, batch, seq, row_off, tq, col_base):
    nq = seq // tq
    off = row_off // tq
    width = H_FOX * DH_FOX
    qb, kb, vb = (col_base + n for n in range(3))

    return pl.pallas_call(
        functools.partial(_fox_prompt_kernel, tq=tq, scale=DH_FOX ** -0.5),
        out_shape=jax.ShapeDtypeStruct((proj.shape[0], width), BF16),
        grid=(batch, nq, nq),
        in_specs=[
            pl.BlockSpec((tq, width), lambda b, i, j: (off + b * nq + i, qb)),
            pl.BlockSpec((tq, width), lambda b, i, j: (off + b * nq + jnp.minimum(i, j), kb)),
            pl.BlockSpec((tq, width), lambda b, i, j: (off + b * nq + jnp.minimum(i, j), vb)),
            pl.BlockSpec((1, H_FOX, tq), lambda b, i, j: (b, 0, i)),
            pl.BlockSpec((1, tq, H_FOX), lambda b, i, j: (b, jnp.minimum(i, j), 0)),
        ],
        out_specs=pl.BlockSpec((tq, width), lambda b, i, j: (off + b * nq + i, 0)),
        scratch_shapes=[pltpu.VMEM((H_FOX, 1, tq), F32), pltpu.VMEM((H_FOX, 1, tq), F32),
                        pltpu.VMEM((H_FOX, DH_FOX, tq), F32)],
        compiler_params=_params(("parallel", "parallel", "arbitrary"), 48),
        name="fox_prompt",
    )(proj, proj, proj, c_row, c_col)


def _softmax_step(h, s, v, m_ref, l_ref, acc_ref):
    m_prev = m_ref[h]
    m_new = jnp.maximum(m_prev, jnp.max(s, axis=-1, keepdims=True))
    p = jnp.exp(s - m_new)
    alpha = jnp.exp(m_prev - m_new)
    l_ref[h] = alpha * l_ref[h] + jnp.sum(p, axis=-1, keepdims=True)
    acc_ref[h] = alpha * acc_ref[h] + jnp.dot(p.astype(BF16), v, preferred_element_type=F32)
    m_ref[h] = m_new


def _fox_sample_kernel(q_ref, kc_ref, vc_ref, kn_ref, vn_ref, cq_ref, ckc_ref, ckn_ref, o_ref,
                       m_ref, l_ref, acc_ref, *, tq, tkc, scale):
    j = pl.program_id(1)
    last = pl.num_programs(1) - 1

    @pl.when(j == 0)
    def _():
        m_ref[...] = jnp.full(m_ref.shape, NEG_INF, F32)
        l_ref[...] = jnp.zeros(l_ref.shape, F32)
        acc_ref[...] = jnp.zeros(acc_ref.shape, F32)

    @pl.when(j < last)
    def _():
        for h in range(H_FOX):
            cols = slice(h * DH_FOX, (h + 1) * DH_FOX)
            q = q_ref[:, cols].astype(BF16)
            k = kc_ref[pl.ds(h, tkc, stride=H_FOX), :].astype(BF16)
            v = vc_ref[pl.ds(h, tkc, stride=H_FOX), :].astype(BF16)
            s = (lax.dot_general(q, k, _NT, preferred_element_type=F32) * scale
                 + (cq_ref[0, :, h:h + 1] - ckc_ref[0, h:h + 1, :]))
            _softmax_step(h, s, v, m_ref, l_ref, acc_ref)

    @pl.when(j == last)
    def _():
        row = lax.broadcasted_iota(jnp.int32, (tq, tq), 0)
        col = lax.broadcasted_iota(jnp.int32, (tq, tq), 1)
        visible = col <= row
        for h in range(H_FOX):
            cols = slice(h * DH_FOX, (h + 1) * DH_FOX)
            q = q_ref[:, cols].astype(BF16)
            k = kn_ref[:, cols].astype(BF16)
            s = (lax.dot_general(q, k, _NT, preferred_element_type=F32) * scale
                 + (cq_ref[0, :, h:h + 1] - ckn_ref[0, h:h + 1, :]))
            s = jnp.where(visible, s, NEG_INF)
            _softmax_step(h, s, vn_ref[:, cols].astype(BF16), m_ref, l_ref, acc_ref)
        for h in range(H_FOX):
            o_ref[:, h * DH_FOX:(h + 1) * DH_FOX] = (acc_ref[h] / l_ref[h]).astype(o_ref.dtype)


def _fox_sample(proj, cache_k, cache_v, c_col, c_row_cache, c_row_new, dest, *, layer, batch, seq,
                row_off, tkc, col_base):
    past = cache_k.shape[2] // H_FOX
    nkc = past // tkc
    off = row_off // seq
    width = H_FOX * DH_FOX
    qb, kb, vb = (col_base + n for n in range(3))
    cached = lambda j: jnp.minimum(j, nkc - 1)
    cache_spec = pl.BlockSpec((None, None, tkc * H_FOX, DH_FOX), lambda b, j: (layer, b, cached(j), 0))

    call = _call_into(
        functools.partial(_fox_sample_kernel, tq=seq, tkc=tkc, scale=DH_FOX ** -0.5), dest, 8,
        out_shape=jax.ShapeDtypeStruct((proj.shape[0], width), BF16),
        grid=(batch, nkc + 1),
        in_specs=[
            pl.BlockSpec((seq, width), lambda b, j: (off + b, qb)),
            cache_spec,
            cache_spec,
            pl.BlockSpec((seq, width), lambda b, j: (off + b, kb)),
            pl.BlockSpec((seq, width), lambda b, j: (off + b, vb)),
            pl.BlockSpec((1, seq, H_FOX), lambda b, j: (b, 0, 0)),
            pl.BlockSpec((1, H_FOX, tkc), lambda b, j: (b, 0, cached(j))),
            pl.BlockSpec((1, H_FOX, seq), lambda b, j: (b, 0, 0)),
        ],
        out_specs=pl.BlockSpec((seq, width), lambda b, j: (off + b, 0)),
        scratch_shapes=[pltpu.VMEM((H_FOX, seq, 1), F32), pltpu.VMEM((H_FOX, seq, 1), F32),
                        pltpu.VMEM((H_FOX, seq, DH_FOX), F32)],
        compiler_params=_params(("parallel", "arbitrary"), 48),
        name="fox_sample",
    )
    return call(proj, cache_k, cache_v, proj, proj, c_col, c_row_cache, c_row_new)


def _out_proj_kernel(a_ref, b_ref, w_ref, x_ref, o_ref):
    ka = a_ref.shape[1]
    o_ref[...] = (x_ref[...]
                  + jnp.dot(a_ref[...], w_ref[:ka, :], preferred_element_type=F32)
                  + jnp.dot(b_ref[...], w_ref[ka:, :], preferred_element_type=F32))


def _out_proj(a, b, w, x, *, layer, tm):
    t, d = x.shape
    ka, kb = a.shape[1], b.shape[1]
    return pl.pallas_call(
        _out_proj_kernel,
        out_shape=jax.ShapeDtypeStruct((t, d), F32),
        grid=(t // tm,),
        in_specs=[
            pl.BlockSpec((tm, ka), lambda i: (i, 0)),
            pl.BlockSpec((tm, kb), lambda i: (i, 0)),
            pl.BlockSpec((None, ka + kb, d), lambda i: (layer, 0, 0)),
            pl.BlockSpec((tm, d), lambda i: (i, 0)),
        ],
        out_specs=pl.BlockSpec((tm, d), lambda i: (i, 0)),
        compiler_params=_params(("parallel",), 48),
        name="out_proj",
    )(a, b, w, x)


def _pool_kernel(x_ref, halo_ref, buf_ref, g_ref, w_ref, ps_ref, o_ref, bo_ref, full_ref, *, tm, pos0):
    i = pl.program_id(1)
    g = g_ref[...]
    x = x_ref[...]
    u = _rms(x, g)
    hist = jnp.where(i == 0, buf_ref[0], _rms(halo_ref[...], g))
    full_ref[0:POOL_HALO, :] = hist
    full_ref[POOL_HALO:POOL_HALO + tm, :] = u
    pos = pos0 + i * tm + lax.broadcasted_iota(jnp.int32, (tm, 1), 0)
    gc = x.shape[1] // len(POOL_WINDOWS)
    for n, win in enumerate(POOL_WINDOWS):
        cols = slice(n * gc, (n + 1) * gc)
        total = u[:, cols]
        for back in range(1, win):
            total = total + full_ref[POOL_HALO - back:POOL_HALO - back + tm, cols]
        inv_cnt = 1.0 / jnp.minimum(win, pos + 1).astype(F32)
        diff = (total * inv_cnt - u[:, cols]).astype(BF16)
        y = jnp.dot(diff, w_ref[n], preferred_element_type=F32) * ps_ref[:, cols]
        o_ref[:, cols] = x[:, cols] + y
    bo_ref[0] = u[tm - POOL_HALO:, :]


def _pool_mixer(x, buf, g, w, ps, dest, *, layer, mixer, batch, seq, row_off, tm, pos0):
    d = x.shape[1]
    nt = seq // tm
    off = row_off // tm
    per_halo = tm // POOL_HALO
    halo_off = row_off // POOL_HALO

    def halo_map(b, i):
        return (jnp.maximum(halo_off + (b * nt + i) * per_halo - 1, 0), 0)

    rows = pl.BlockSpec((tm, d), lambda b, i: (off + b * nt + i, 0))
    call = _call_into(
        functools.partial(_pool_kernel, tm=tm, pos0=pos0), dest, 6,
        out_shape=(jax.ShapeDtypeStruct(x.shape, F32),
                   jax.ShapeDtypeStruct((batch, POOL_HALO, d), F32)),
        grid=(batch, nt),
        in_specs=[
            rows,
            pl.BlockSpec((POOL_HALO, d), halo_map),
            pl.BlockSpec((1, POOL_HALO, d), lambda b, i: (b, 0, 0)),
            _layer_vec(layer, d),
            pl.BlockSpec((None,) + w.shape[1:], lambda b, i: (mixer, 0, 0, 0)),
            _layer_vec(mixer, d),
        ],
        out_specs=(rows, pl.BlockSpec((1, POOL_HALO, d), lambda b, i: (b, 0, 0))),
        scratch_shapes=[pltpu.VMEM((POOL_HALO + tm, d), F32)],
        compiler_params=_params(("parallel", "arbitrary"), 48),
        name="pool_mixer",
    )
    return call(x, x, buf, g, w, ps)


def _ffn_kernel(x_ref, g_ref, wg_ref, wu_ref, wd_ref, o_ref, h_ref):
    @pl.when(pl.program_id(1) == 0)
    def _():
        x = x_ref[...]
        h_ref[...] = _rms(x, g_ref[...]).astype(h_ref.dtype)
        o_ref[...] = x

    h = h_ref[...]
    a = jnp.dot(h, wg_ref[...], preferred_element_type=F32)
    u = jnp.dot(h, wu_ref[...], preferred_element_type=F32)
    act = (a * _sigmoid(a) * u).astype(BF16)
    o_ref[...] += jnp.dot(act, wd_ref[...], preferred_element_type=F32)


def _ffn(x, g, wg, wu, wd, *, layer, tm, tf):
    t, d = x.shape
    f = wg.shape[2]
    return pl.pallas_call(
        _ffn_kernel,
        out_shape=jax.ShapeDtypeStruct((t, d), F32),
        grid=(t // tm, f // tf),
        in_specs=[
            pl.BlockSpec((tm, d), lambda i, j: (i, 0)),
            _layer_vec(layer, d),
            pl.BlockSpec((None, d, tf), lambda i, j: (layer, 0, j)),
            pl.BlockSpec((None, d, tf), lambda i, j: (layer, 0, j)),
            pl.BlockSpec((None, tf, d), lambda i, j: (layer, j, 0)),
        ],
        out_specs=pl.BlockSpec((tm, d), lambda i, j: (i, 0)),
        scratch_shapes=[pltpu.VMEM((tm, d), BF16)],
        compiler_params=_params(("parallel", "arbitrary"), 48),
        name="swiglu_ffn",
    )(x, g, wg, wu, wd)


def _ple_kernel(x_ref, g_ref, wg_ref, p_ref, wp_ref, o_ref):
    x = x_ref[...]
    h = _rms(x, g_ref[...]).astype(BF16)
    gate = _sigmoid(jnp.dot(h, wg_ref[...], preferred_element_type=F32))
    emb = jnp.dot(p_ref[...].astype(BF16), wp_ref[...], preferred_element_type=F32)
    o_ref[...] = x + gate * emb


def _ple(x, g, wg, p, wp, *, layer, tm):
    t, d = x.shape
    pd = p.shape[2]
    return pl.pallas_call(
        _ple_kernel,
        out_shape=jax.ShapeDtypeStruct((t, d), F32),
        grid=(t // tm,),
        in_specs=[
            pl.BlockSpec((tm, d), lambda i: (i, 0)),
            _layer_vec(layer, d),
            pl.BlockSpec((None, d, d), lambda i: (layer, 0, 0)),
            pl.BlockSpec((None, tm, pd), lambda i: (layer, i, 0)),
            pl.BlockSpec((None, pd, d), lambda i: (layer, 0, 0)),
        ],
        out_specs=pl.BlockSpec((tm, d), lambda i: (i, 0)),
        compiler_params=_params(("parallel",), 56),
        name="gated_embedding",
    )(x, g, wg, p, wp)


def _final_norm_kernel(x_ref, g_ref, o_ref):
    o_ref[...] = _rms(x_ref[...], g_ref[...])


def _final_norm(x, g, *, row_off, rows, tm):
    d = x.shape[1]
    off = row_off // tm
    return pl.pallas_call(
        _final_norm_kernel,
        out_shape=jax.ShapeDtypeStruct((rows, d), F32),
        grid=(rows // tm,),
        in_specs=[pl.BlockSpec((tm, d), lambda i: (off + i, 0)), pl.BlockSpec((1, d), lambda i: (0, 0))],
        out_specs=pl.BlockSpec((tm, d), lambda i: (i, 0)),
        compiler_params=_params(("parallel",), 32),
        name="final_norm",
    )(x, g)


def _rope_tables(pos0, seq):
    half = DK_RET // 2
    inv = ROPE_BASE ** (-jnp.arange(half, dtype=F32) / half)
    ang = (pos0 + jnp.arange(seq, dtype=jnp.int32)).astype(F32)[:, None] * inv[None, :]
    return jnp.cos(ang), jnp.sin(ang)


def _decay_tables():
    log_g = jnp.log1p(-jnp.power(2.0, -5.0 - jnp.arange(H_RET, dtype=F32)))
    idx = jnp.arange(CHUNK, dtype=F32)
    intra = jnp.exp(log_g[:, None, None] * jnp.abs(idx[:, None] - idx[None, :]))
    q_dec = jnp.exp(log_g[:, None] * (idx[None, :] + 1.0))[..., None]
    k_dec = jnp.exp(log_g[:, None] * (CHUNK - 1.0 - idx[None, :]))[..., None]
    s_dec = jnp.exp(log_g * CHUNK)[:, None, None]
    return intra, q_dec, k_dec, s_dec


def _tile(n, want):
    t = min(n, want)
    while n % t:
        t //= 2
    return t


def kernel(x_prompt, x_sample, p_prompt, p_sample, state_ret, cache_fox_k, cache_fox_v, cache_fox_logf,
           state_pool, norm_mix, w_in, b_forget, w_out, w_pool, pool_scale, norm_ffn, w_gate, w_up, w_down,
           norm_ple, ple_gate, ple_proj, final_norm):
    bp, tp, d = x_prompt.shape
    bs, ts, _ = x_sample.shape
    depth = norm_mix.shape[0]
    past = cache_fox_k.shape[2]
    n_p, n_s = bp * tp, bs * ts
    n_tok = n_p + n_s
    ret_w = H_RET * DK_RET
    fox_w = H_FOX * DH_FOX
    main_w = 4 * ret_w + 3 * fox_w
    fox_base = (4 * ret_w) // fox_w

    x = jnp.concatenate([x_prompt.reshape(n_p, d), x_sample.reshape(n_s, d)], axis=0)
    p_all = jnp.concatenate([p_prompt.reshape(depth, n_p, -1), p_sample.reshape(depth, n_s, -1)], axis=1)
    tm = _tile(n_tok, 512)

    as_rows = lambda a: a.reshape(a.shape[0], 1, a.shape[1])
    norm_mix, norm_ffn, norm_ple, pool_scale = map(as_rows, (norm_mix, norm_ffn, norm_ple, pool_scale))
    w_in_b, w_out_b, w_pool_b = w_in.astype(BF16), w_out.astype(BF16), w_pool.astype(BF16)
    w_gate_b, w_up_b, w_down_b = w_gate.astype(BF16), w_up.astype(BF16), w_down.astype(BF16)
    ple_gate_b, ple_proj_b = ple_gate.astype(BF16), ple_proj.astype(BF16)
    w_f = jnp.pad(w_in[:, :, main_w:], ((0, 0), (0, 0), (0, LANES - H_FOX))).astype(BF16)
    b_f = jnp.pad(b_forget.astype(F32), ((0, 0), (0, LANES - H_FOX)))[:, None, :]
    cache_k = cache_fox_k.reshape(cache_fox_k.shape[:2] + (past * H_FOX, DH_FOX))
    cache_v = cache_fox_v.reshape(cache_fox_v.shape[:2] + (past * H_FOX, DH_FOX))

    decay = _decay_tables()
    rope_p = _rope_tables(0, tp)
    rope_s = _rope_tables(past, ts)
    zero_state = jnp.zeros((1, bp) + state_ret.shape[2:], F32)
    zero_buf = jnp.zeros((bp, POOL_HALO, d), F32)

    new_ret_p, new_ret_s, new_k, new_v, new_lf, new_pool_p, new_pool_s = [], [], [], [], [], [], []
    for i in range(depth):
        if i % 2 == 0:
            e = i // 2
            proj = _norm_matmul(x, norm_mix, w_in_b, layer=i, wlayer=e, n=main_w, tm=_tile(n_tok, 1024),
                                tn=512)
            lf = _forget_gate(x, norm_mix, w_f, b_f, layer=i, wlayer=e, tm=tm)[:, :H_FOX]
            lf_p = lf[:n_p].reshape(bp, tp, H_FOX)
            lf_s = lf[n_p:].reshape(bs, ts, H_FOX)
            c_row_p = _cumsum_rows(jnp.transpose(lf_p, (0, 2, 1)))
            lf_all_s = jnp.concatenate([cache_fox_logf[e].astype(F32), lf_s], axis=1)
            c_row_s = _cumsum_rows(jnp.transpose(lf_all_s, (0, 2, 1)))
            c_col_p = jnp.transpose(c_row_p, (0, 2, 1))
            c_col_s = jnp.transpose(c_row_s[:, :, past:], (0, 2, 1))

            o_r, s_p = _retention(proj, *rope_p, decay, zero_state, None, layer=0, batch=bp, seq=tp,
                                  row_off=0, chunks=_tile(tp, 512) // CHUNK)
            o_r, s_s = _retention(proj, *rope_s, decay, state_ret.astype(F32), o_r, layer=e, batch=bs,
                                  seq=ts, row_off=n_p, chunks=ts // CHUNK)
            o_f = _fox_prompt(proj, c_col_p, c_row_p, batch=bp, seq=tp, row_off=0, tq=_tile(tp, 512),
                              col_base=fox_base)
            o_f = _fox_sample(proj, cache_k, cache_v, c_col_s, c_row_s, c_row_s[:, :, past:], o_f,
                              layer=e, batch=bs, seq=ts, row_off=n_p, tkc=_tile(past, 1024),
                              col_base=fox_base)
            x = _out_proj(o_r, o_f, w_out_b, x, layer=e, tm=tm)

            k_f = proj[:, 4 * ret_w + fox_w:4 * ret_w + 2 * fox_w]
            v_f = proj[:, 4 * ret_w + 2 * fox_w:main_w]
            new_ret_p.append(s_p)
            new_ret_s.append(s_s.astype(state_ret.dtype))
            new_k.append(k_f)
            new_v.append(v_f)
            new_lf.append((lf_p, lf_s))
        else:
            o = i // 2
            buf_s = jnp.pad(state_pool[o].astype(F32), ((0, 0), (1, 0), (0, 0)))
            x_new, pool_p = _pool_mixer(x, zero_buf, norm_mix, w_pool_b, pool_scale, None, layer=i, mixer=o,
                                        batch=bp, seq=tp, row_off=0, tm=_tile(tp, 512), pos0=0)
            x, pool_s = _pool_mixer(x, buf_s, norm_mix, w_pool_b, pool_scale, x_new, layer=i, mixer=o,
                                    batch=bs, seq=ts, row_off=n_p, tm=ts, pos0=past)
            new_pool_p.append(pool_p[:, 1:])
            new_pool_s.append(pool_s[:, 1:])
        x = _ffn(x, norm_ffn, w_gate_b, w_up_b, w_down_b, layer=i, tm=tm, tf=512)
        x = _ple(x, norm_ple, ple_gate_b, p_all, ple_proj_b, layer=i, tm=_tile(n_tok, 256))

    g_final = final_norm[None, :]
    y_p = _final_norm(x, g_final, row_off=0, rows=n_p, tm=_tile(n_p, 512))
    y_s = _final_norm(x, g_final, row_off=n_p, rows=n_s, tm=_tile(math.gcd(n_p, n_s), 512))
    heads = (H_FOX, DH_FOX)
    return (
        y_p.reshape(bp, tp, d),
        y_s.reshape(bs, ts, d),
        jnp.stack(new_ret_p),
        jnp.stack(new_ret_s),
        jnp.stack([k[:n_p].reshape((bp, tp) + heads) for k in new_k]),
        jnp.stack([k[n_p:].reshape((bs, ts) + heads) for k in new_k]),
        jnp.stack([v[:n_p].reshape((bp, tp) + heads) for v in new_v]),
        jnp.stack([v[n_p:].reshape((bs, ts) + heads) for v in new_v]),
        jnp.stack([lf[0] for lf in new_lf]),
        jnp.stack([lf[1] for lf in new_lf]),
        jnp.stack(new_pool_p),
        jnp.stack(new_pool_s),
    )
```

```python
import functools
import math

import jax
import jax.numpy as jnp
from jax import lax
from jax.experimental import pallas as pl
from jax.experimental.pallas import tpu as pltpu

F32 = jnp.float32
BF16 = jnp.bfloat16

EPS = 1e-6
NEG_INF = -1e30
ROPE_BASE = 10000.0
LOG2E = math.log2(math.e)
CHUNK = 64
H_RET = 4
DK_RET = 256
H_FOX = 8
DH_FOX = 128
POOL_WINDOWS = (2, 4, 8, 16)
POOL_HALO = 16
LANES = 128
CUMSUM_CHUNK = 256

_NT = (((1,), (1,)), ((), ()))
_TN = (((0,), (0,)), ((), ()))


def _params(semantics, vmem_mib):
    return pltpu.CompilerParams(dimension_semantics=semantics, vmem_limit_bytes=vmem_mib << 20)


def _rms(x, g):
    return x * lax.rsqrt(jnp.mean(x * x, axis=-1, keepdims=True) + EPS) * g


def _sigmoid(x):
    return 1.0 / (1.0 + jnp.exp(-x))


def _layer_vec(layer, d):
    return pl.BlockSpec((None, 1, d), lambda *_: (layer, 0, 0))


def _call_into(kernel, dests, n_in, **kwargs):
    dests = {k: a for k, a in dests.items() if a is not None}
    if not dests:
        return pl.pallas_call(kernel, **kwargs)
    n_d = len(dests)

    def body(*refs):
        kernel(*refs[:n_in], *refs[n_in + n_d:])

    kwargs["in_specs"] = list(kwargs["in_specs"]) + [pl.BlockSpec(memory_space=pl.ANY)] * n_d
    aliases = {n_in + pos: out for pos, out in enumerate(dests)}
    call = pl.pallas_call(body, input_output_aliases=aliases, **kwargs)
    return lambda *args: call(*args, *dests.values())


def _in_proj_kernel(x_ref, g_ref, w_ref, o_ref, kp_ref, ks_ref, vp_ref, vs_ref, h_ref, *, tm, jk, ntp):
    i = pl.program_id(0)
    j = pl.program_id(1)

    @pl.when(j == 0)
    def _():
        h_ref[...] = _rms(x_ref[...], g_ref[...]).astype(h_ref.dtype)

    o_ref[...] = jnp.dot(h_ref[...], w_ref[...], preferred_element_type=F32)

    def to_head_rows(dst_ref):
        for h in range(H_FOX):
            dst_ref[pl.ds(h, tm, stride=H_FOX), :] = o_ref[:, h * DH_FOX:(h + 1) * DH_FOX]

    for jj, prompt_ref, sample_ref in ((jk, kp_ref, ks_ref), (jk + 1, vp_ref, vs_ref)):
        @pl.when((j == jj) & (i < ntp))
        def _():
            to_head_rows(prompt_ref)

        @pl.when((j == jj) & (i >= ntp))
        def _():
            to_head_rows(sample_ref)


def _norm_matmul_logsig_kernel(x_ref, g_ref, w_ref, b_ref, o_ref):
    h = _rms(x_ref[...], g_ref[...]).astype(BF16)
    z = jnp.dot(h, w_ref[...], preferred_element_type=F32) + b_ref[...]
    o_ref[...] = jnp.minimum(z, 0.0) - jnp.log1p(jnp.exp(-jnp.abs(z)))


def _in_proj(x, g, w, kv_dests, *, layer, wlayer, n_layers, n, n_p, k_col, tm):
    t, d = x.shape
    tn = H_FOX * DH_FOX
    ntp, nts = n_p // tm, (t - n_p) // tm
    rows = tm * H_FOX
    prompt_rows = pl.BlockSpec((rows, DH_FOX), lambda i, j: (wlayer * ntp + jnp.minimum(i, ntp - 1), 0))
    sample_rows = pl.BlockSpec((rows, DH_FOX),
                               lambda i, j: (wlayer * nts + jnp.clip(i - ntp, 0, nts - 1), 0))
    kv_p = jax.ShapeDtypeStruct((n_layers * n_p * H_FOX, DH_FOX), F32)
    kv_s = jax.ShapeDtypeStruct((n_layers * (t - n_p) * H_FOX, DH_FOX), F32)
    call = _call_into(
        functools.partial(_in_proj_kernel, tm=tm, jk=k_col // tn, ntp=ntp),
        {1 + n: a for n, a in enumerate(kv_dests)}, 3,
        out_shape=(jax.ShapeDtypeStruct((t, n), F32), kv_p, kv_s, kv_p, kv_s),
        grid=(t // tm, n // tn),
        in_specs=[
            pl.BlockSpec((tm, d), lambda i, j: (i, 0)),
            _layer_vec(layer, d),
            pl.BlockSpec((None, d, tn), lambda i, j: (wlayer, 0, j)),
        ],
        out_specs=(pl.BlockSpec((tm, tn), lambda i, j: (i, j)),
                   prompt_rows, sample_rows, prompt_rows, sample_rows),
        scratch_shapes=[pltpu.VMEM((tm, d), BF16)],
        compiler_params=_params(("arbitrary", "arbitrary"), 48),
        name="in_proj",
    )
    return call(x, g, w)


def _forget_gate(x, g, w, b, *, layer, wlayer, tm):
    t, d = x.shape
    n = w.shape[2]
    return pl.pallas_call(
        _norm_matmul_logsig_kernel,
        out_shape=jax.ShapeDtypeStruct((t, n), F32),
        grid=(t // tm,),
        in_specs=[
            pl.BlockSpec((tm, d), lambda i: (i, 0)),
            _layer_vec(layer, d),
            pl.BlockSpec((None, d, n), lambda i: (wlayer, 0, 0)),
            pl.BlockSpec((None, 1, n), lambda i: (wlayer, 0, 0)),
        ],
        out_specs=pl.BlockSpec((tm, n), lambda i: (i, 0)),
        compiler_params=_params(("parallel",), 32),
        name="forget_gate",
    )(x, g, w, b)


def _cumsum_kernel(lf_ref, c_ref, *, tk):
    r = lax.broadcasted_iota(jnp.int32, (CUMSUM_CHUNK, CUMSUM_CHUNK), 0)
    c = lax.broadcasted_iota(jnp.int32, (CUMSUM_CHUNK, CUMSUM_CHUNK), 1)
    upper = (r <= c).astype(F32)
    carry = jnp.zeros((H_FOX, 1), F32)
    for s in range(0, tk, CUMSUM_CHUNK):
        w = min(CUMSUM_CHUNK, tk - s)
        blk = lf_ref[0, :, s:s + w]
        cs = jnp.dot(blk, upper[:w, :w], precision=lax.Precision.HIGHEST,
                     preferred_element_type=F32) + carry
        c_ref[0, :, s:s + w] = cs
        carry = cs[:, w - 1:w]


def _cumsum_rows(lf_rows):
    b, h, tk = lf_rows.shape
    return pl.pallas_call(
        functools.partial(_cumsum_kernel, tk=tk),
        out_shape=jax.ShapeDtypeStruct((b, h, tk), F32),
        grid=(b,),
        in_specs=[pl.BlockSpec((1, h, tk), lambda i: (i, 0, 0))],
        out_specs=pl.BlockSpec((1, h, tk), lambda i: (i, 0, 0)),
        compiler_params=_params(("parallel",), 32),
        name="logf_cumsum",
    )(lf_rows)


def _rope(x, cos, sin):
    half = x.shape[-1] // 2
    x1, x2 = x[:, :half], x[:, half:]
    return jnp.concatenate([x1 * cos - x2 * sin, x1 * sin + x2 * cos], axis=-1)


def _retention_kernel(q_ref, k_ref, v_ref, g_ref, cos_ref, sin_ref, intra_ref, qdec_ref, kdec_ref,
                      sdec_ref, s0_ref, o_ref, s_out_ref, state_ref, *, chunks, scale):
    i = pl.program_id(2)

    @pl.when(i == 0)
    def _():
        state_ref[...] = s0_ref[0, 0]

    intra = intra_ref[0]
    q_dec = qdec_ref[0]
    k_dec = kdec_ref[0]
    s_dec = sdec_ref[0]
    for c in range(chunks):
        rows = slice(c * CHUNK, (c + 1) * CHUNK)
        cos = cos_ref[rows, :]
        sin = sin_ref[rows, :]
        q = _rope(q_ref[rows, :], cos, sin)
        k = _rope(k_ref[rows, :], cos, sin) * scale
        qb = q.astype(BF16)
        vb = v_ref[rows, :].astype(BF16)
        state = state_ref[...]
        sc = lax.dot_general(qb, k.astype(BF16), _NT, preferred_element_type=F32) * intra
        o = (jnp.dot(sc.astype(BF16), vb, preferred_element_type=F32)
             + jnp.dot(qb, state.astype(BF16), preferred_element_type=F32) * q_dec)
        kd = (k * k_dec).astype(BF16)
        state_ref[...] = state * s_dec + lax.dot_general(kd, vb, _TN, preferred_element_type=F32)
        o = o * lax.rsqrt(jnp.mean(o * o, axis=-1, keepdims=True) + EPS)
        gate = g_ref[rows, :]
        o_ref[rows, :] = (o * (gate * _sigmoid(gate))).astype(o_ref.dtype)

    @pl.when(i == pl.num_programs(2) - 1)
    def _():
        s_out_ref[0, 0] = state_ref[...]


def _retention(proj, cos, sin, decay, state, dest, state_dest, *, layer, out_layer, n_layers, batch, seq,
               row_off, chunks):
    tblk = chunks * CHUNK
    nblk = seq // tblk
    off = row_off // tblk
    intra, q_dec, k_dec, s_dec = decay

    def col(base):
        return pl.BlockSpec((tblk, DK_RET), lambda b, h, i: (off + b * nblk + i, base + h))

    per_head = lambda shape: pl.BlockSpec((1,) + shape, lambda b, h, i: (h, 0, 0))
    table = pl.BlockSpec((tblk, DK_RET // 2), lambda b, h, i: (i, 0))
    state_block = (None, 1, 1, DK_RET, DK_RET)
    call = _call_into(
        functools.partial(_retention_kernel, chunks=chunks, scale=DK_RET ** -0.5),
        {0: dest, 1: state_dest}, 11,
        out_shape=(jax.ShapeDtypeStruct((proj.shape[0], H_RET * DK_RET), BF16),
                   jax.ShapeDtypeStruct((n_layers, batch, H_RET, DK_RET, DK_RET), F32)),
        grid=(batch, H_RET, nblk),
        in_specs=[col(0), col(H_RET), col(2 * H_RET), col(3 * H_RET), table, table,
                  per_head((CHUNK, CHUNK)), per_head((CHUNK, 1)), per_head((CHUNK, 1)),
                  per_head((1, 1)),
                  pl.BlockSpec(state_block, lambda b, h, i: (layer, b, h, 0, 0))],
        out_specs=(col(0), pl.BlockSpec(state_block, lambda b, h, i: (out_layer, b, h, 0, 0))),
        scratch_shapes=[pltpu.VMEM((DK_RET, DK_RET), F32)],
        compiler_params=_params(("parallel", "parallel", "arbitrary"), 32),
        name="retention",
    )
    return call(proj, proj, proj, proj, cos, sin, intra, q_dec, k_dec, s_dec, state)


def _fox_prompt_kernel(q_ref, k_ref, v_ref, cq_ref, ck_ref, o_ref, m_ref, l_ref, acc_ref, *, tq, scale):
    qi = pl.program_id(1)
    kj = pl.program_id(2)

    @pl.when(kj == 0)
    def _():
        m_ref[...] = jnp.full(m_ref.shape, NEG_INF, F32)
        l_ref[...] = jnp.zeros(l_ref.shape, F32)
        acc_ref[...] = jnp.zeros(acc_ref.shape, F32)

    def step(masked):
        if masked:
            key = lax.broadcasted_iota(jnp.int32, (tq, tq), 0)
            qry = lax.broadcasted_iota(jnp.int32, (tq, tq), 1)
            visible = key <= qry
        for h in range(H_FOX):
            cols = slice(h * DH_FOX, (h + 1) * DH_FOX)
            q = (q_ref[:, cols] * (scale * LOG2E)).astype(BF16)
            k = k_ref[:, cols].astype(BF16)
            bias = cq_ref[0, h:h + 1, :] * LOG2E - ck_ref[0, :, h:h + 1] * LOG2E
            s = lax.dot_general(k, q, _NT, preferred_element_type=F32) + bias
            if masked:
                s = jnp.where(visible, s, NEG_INF)
            m_prev = m_ref[h]
            m_new = jnp.maximum(m_prev, jnp.max(s, axis=0, keepdims=True))
            p = jnp.exp2(s - m_new)
            alpha = jnp.exp2(m_prev - m_new)
            l_ref[h] = alpha * l_ref[h] + jnp.sum(p, axis=0, keepdims=True)
            pv = lax.dot_general(v_ref[:, cols].astype(BF16), p.astype(BF16), _TN,
                                 preferred_element_type=F32)
            acc_ref[h] = alpha * acc_ref[h] + pv
            m_ref[h] = m_new

    @pl.when(kj < qi)
    def _():
        step(False)

    @pl.when(kj == qi)
    def _():
        step(True)
        for h in range(H_FOX):
            o = (acc_ref[h] / l_ref[h]).T
            o_ref[:, h * DH_FOX:(h + 1) * DH_FOX] = o.astype(o_ref.dtype)


def _fox_prompt(proj, c_col, c_row, *, batch, seq, row_off, tq, col_base):
    nq = seq // tq
    off = row_off // tq
    width = H_FOX * DH_FOX
    qb, kb, vb = (col_base + n for n in range(3))

    return pl.pallas_call(
        functools.partial(_fox_prompt_kernel, tq=tq, scale=DH_FOX ** -0.5),
        out_shape=jax.ShapeDtypeStruct((proj.shape[0], width), BF16),
        grid=(batch, nq, nq),
        in_specs=[
            pl.BlockSpec((tq, width), lambda b, i, j: (off + b * nq + i, qb)),
            pl.BlockSpec((tq, width), lambda b, i, j: (off + b * nq + jnp.minimum(i, j), kb)),
            pl.BlockSpec((tq, width), lambda b, i, j: (off + b * nq + jnp.minimum(i, j), vb)),
            pl.BlockSpec((1, H_FOX, tq), lambda b, i, j: (b, 0, i)),
            pl.BlockSpec((1, tq, H_FOX), lambda b, i, j: (b, jnp.minimum(i, j), 0)),
        ],
        out_specs=pl.BlockSpec((tq, width), lambda b, i, j: (off + b * nq + i, 0)),
        scratch_shapes=[pltpu.VMEM((H_FOX, 1, tq), F32), pltpu.VMEM((H_FOX, 1, tq), F32),
                        pltpu.VMEM((H_FOX, DH_FOX, tq), F32)],
        compiler_params=_params(("parallel", "parallel", "arbitrary"), 48),
        name="fox_prompt",
    )(proj, proj, proj, c_row, c_col)


def _fox_sample_kernel(q_ref, kc_ref, vc_ref, kn_ref, vn_ref, cq_ref, ckc_ref, ckn_ref, o_ref,
                       m_ref, l_ref, acc_ref, s_ref, p_ref, *, tq, tkc, scale):
    j = pl.program_id(1)
    last = pl.num_programs(1) - 1

    @pl.when(j == 0)
    def _():
        m_ref[...] = jnp.full(m_ref.shape, NEG_INF, F32)
        l_ref[...] = jnp.zeros(l_ref.shape, F32)
        acc_ref[...] = jnp.zeros(acc_ref.shape, F32)

    cq = cq_ref[0] * LOG2E

    def scores(h, k, ck, visible=None):
        q = (q_ref[:, h * DH_FOX:(h + 1) * DH_FOX] * (scale * LOG2E)).astype(BF16)
        s = (lax.dot_general(q, k, _NT, preferred_element_type=F32)
             + (cq[:, h:h + 1] - ck[h:h + 1, :]))
        if visible is not None:
            s = jnp.where(visible, s, NEG_INF)
        s_ref[h * tq:(h + 1) * tq, :s.shape[1]] = s

    def softmax_update(width):
        s = s_ref[:, :width]
        m_prev = m_ref[...]
        m_new = jnp.maximum(m_prev, jnp.max(s, axis=-1, keepdims=True))
        p = jnp.exp2(s - m_new)
        alpha = jnp.exp2(m_prev - m_new)
        l_ref[...] = alpha * l_ref[...] + jnp.sum(p, axis=-1, keepdims=True)
        m_ref[...] = m_new
        p_ref[:, :width] = p.astype(BF16)
        return alpha

    def values(h, v, alpha, width):
        rows = slice(h * tq, (h + 1) * tq)
        acc_ref[h] = alpha[rows] * acc_ref[h] + jnp.dot(p_ref[rows, :width], v, preferred_element_type=F32)

    @pl.when(j < last)
    def _():
        ck = ckc_ref[0] * LOG2E
        for h in range(H_FOX):
            scores(h, kc_ref[pl.ds(h, tkc, stride=H_FOX), :].astype(BF16), ck)
        alpha = softmax_update(tkc)
        for h in range(H_FOX):
            values(h, vc_ref[pl.ds(h, tkc, stride=H_FOX), :].astype(BF16), alpha, tkc)

    @pl.when(j == last)
    def _():
        row = lax.broadcasted_iota(jnp.int32, (tq, tq), 0)
        col = lax.broadcasted_iota(jnp.int32, (tq, tq), 1)
        visible = col <= row
        ck = ckn_ref[0] * LOG2E
        for h in range(H_FOX):
            scores(h, kn_ref[:, h * DH_FOX:(h + 1) * DH_FOX].astype(BF16), ck, visible)
        alpha = softmax_update(tq)
        for h in range(H_FOX):
            cols = slice(h * DH_FOX, (h + 1) * DH_FOX)
            values(h, vn_ref[:, cols].astype(BF16), alpha, tq)
            o_ref[:, cols] = (acc_ref[h] / l_ref[h * tq:(h + 1) * tq]).astype(o_ref.dtype)


def _fox_sample(proj, cache_k, cache_v, c_col, c_row_cache, c_row_new, dest, *, layer, batch, seq,
                row_off, tkc, col_base):
    past = cache_k.shape[2] // H_FOX
    nkc = past // tkc
    off = row_off // seq
    width = H_FOX * DH_FOX
    qb, kb, vb = (col_base + n for n in range(3))
    cached = lambda j: jnp.minimum(j, nkc - 1)
    cache_spec = pl.BlockSpec((None, None, tkc * H_FOX, DH_FOX), lambda b, j: (layer, b, cached(j), 0))

    call = _call_into(
        functools.partial(_fox_sample_kernel, tq=seq, tkc=tkc, scale=DH_FOX ** -0.5), {0: dest}, 8,
        out_shape=jax.ShapeDtypeStruct((proj.shape[0], width), BF16),
        grid=(batch, nkc + 1),
        in_specs=[
            pl.BlockSpec((seq, width), lambda b, j: (off + b, qb)),
            cache_spec,
            cache_spec,
            pl.BlockSpec((seq, width), lambda b, j: (off + b, kb)),
            pl.BlockSpec((seq, width), lambda b, j: (off + b, vb)),
            pl.BlockSpec((1, seq, H_FOX), lambda b, j: (b, 0, 0)),
            pl.BlockSpec((1, H_FOX, tkc), lambda b, j: (b, 0, cached(j))),
            pl.BlockSpec((1, H_FOX, seq), lambda b, j: (b, 0, 0)),
        ],
        out_specs=pl.BlockSpec((seq, width), lambda b, j: (off + b, 0)),
        scratch_shapes=[pltpu.VMEM((H_FOX * seq, 1), F32), pltpu.VMEM((H_FOX * seq, 1), F32),
                        pltpu.VMEM((H_FOX, seq, DH_FOX), F32),
                        pltpu.VMEM((H_FOX * seq, tkc), F32), pltpu.VMEM((H_FOX * seq, tkc), BF16)],
        compiler_params=_params(("parallel", "arbitrary"), 48),
        name="fox_sample",
    )
    return call(proj, cache_k, cache_v, proj, proj, c_col, c_row_cache, c_row_new)


def _out_proj_kernel(a_ref, b_ref, w_ref, x_ref, o_ref):
    ka = a_ref.shape[1]
    o_ref[...] = (x_ref[...]
                  + jnp.dot(a_ref[...], w_ref[:ka, :], preferred_element_type=F32)
                  + jnp.dot(b_ref[...], w_ref[ka:, :], preferred_element_type=F32))


def _out_proj(a, b, w, x, *, layer, tm):
    t, d = x.shape
    ka, kb = a.shape[1], b.shape[1]
    return pl.pallas_call(
        _out_proj_kernel,
        out_shape=jax.ShapeDtypeStruct((t, d), F32),
        grid=(t // tm,),
        in_specs=[
            pl.BlockSpec((tm, ka), lambda i: (i, 0)),
            pl.BlockSpec((tm, kb), lambda i: (i, 0)),
            pl.BlockSpec((None, ka + kb, d), lambda i: (layer, 0, 0)),
            pl.BlockSpec((tm, d), lambda i: (i, 0)),
        ],
        out_specs=pl.BlockSpec((tm, d), lambda i: (i, 0)),
        compiler_params=_params(("parallel",), 48),
        name="out_proj",
    )(a, b, w, x)


def _pool_kernel(x_ref, halo_ref, buf_ref, g_ref, w_ref, ps_ref, o_ref, bo_ref, full_ref, *, tm, pos0):
    i = pl.program_id(1)
    g = g_ref[...]
    x = x_ref[...]
    u = _rms(x, g)
    hist = jnp.where(i == 0, buf_ref[0], _rms(halo_ref[...], g))
    full_ref[0:POOL_HALO, :] = hist
    full_ref[POOL_HALO:POOL_HALO + tm, :] = u
    pos = pos0 + i * tm + lax.broadcasted_iota(jnp.int32, (tm, 1), 0)
    gc = x.shape[1] // len(POOL_WINDOWS)
    for n, win in enumerate(POOL_WINDOWS):
        cols = slice(n * gc, (n + 1) * gc)
        total = u[:, cols]
        for back in range(1, win):
            total = total + full_ref[POOL_HALO - back:POOL_HALO - back + tm, cols]
        inv_cnt = 1.0 / jnp.minimum(win, pos + 1).astype(F32)
        diff = (total * inv_cnt - u[:, cols]).astype(BF16)
        y = jnp.dot(diff, w_ref[n], preferred_element_type=F32) * ps_ref[:, cols]
        o_ref[:, cols] = x[:, cols] + y
    bo_ref[0] = u[tm - POOL_HALO:, :]


def _pool_mixer(x, buf, g, w, ps, dest, *, layer, mixer, batch, seq, row_off, tm, pos0):
    d = x.shape[1]
    nt = seq // tm
    off = row_off // tm
    per_halo = tm // POOL_HALO
    halo_off = row_off // POOL_HALO

    def halo_map(b, i):
        return (jnp.maximum(halo_off + (b * nt + i) * per_halo - 1, 0), 0)

    rows = pl.BlockSpec((tm, d), lambda b, i: (off + b * nt + i, 0))
    call = _call_into(
        functools.partial(_pool_kernel, tm=tm, pos0=pos0), {0: dest}, 6,
        out_shape=(jax.ShapeDtypeStruct(x.shape, F32),
                   jax.ShapeDtypeStruct((batch, POOL_HALO, d), F32)),
        grid=(batch, nt),
        in_specs=[
            rows,
            pl.BlockSpec((POOL_HALO, d), halo_map),
            pl.BlockSpec((1, POOL_HALO, d), lambda b, i: (b, 0, 0)),
            _layer_vec(layer, d),
            pl.BlockSpec((None,) + w.shape[1:], lambda b, i: (mixer, 0, 0, 0)),
            _layer_vec(mixer, d),
        ],
        out_specs=(rows, pl.BlockSpec((1, POOL_HALO, d), lambda b, i: (b, 0, 0))),
        scratch_shapes=[pltpu.VMEM((POOL_HALO + tm, d), F32)],
        compiler_params=_params(("parallel", "arbitrary"), 48),
        name="pool_mixer",
    )
    return call(x, x, buf, g, w, ps)


def _ffn_kernel(x_ref, g_ref, wg_ref, wu_ref, wd_ref, o_ref, h_ref):
    @pl.when(pl.program_id(1) == 0)
    def _():
        x = x_ref[...]
        h_ref[...] = _rms(x, g_ref[...]).astype(h_ref.dtype)
        o_ref[...] = x

    h = h_ref[...]
    a = jnp.dot(h, wg_ref[...], preferred_element_type=F32)
    u = jnp.dot(h, wu_ref[...], preferred_element_type=F32)
    act = (a * _sigmoid(a) * u).astype(BF16)
    o_ref[...] += jnp.dot(act, wd_ref[...], preferred_element_type=F32)


def _ffn(x, g, wg, wu, wd, *, layer, tm, tf):
    t, d = x.shape
    f = wg.shape[2]
    return pl.pallas_call(
        _ffn_kernel,
        out_shape=jax.ShapeDtypeStruct((t, d), F32),
        grid=(t // tm, f // tf),
        in_specs=[
            pl.BlockSpec((tm, d), lambda i, j: (i, 0)),
            _layer_vec(layer, d),
            pl.BlockSpec((None, d, tf), lambda i, j: (layer, 0, j)),
            pl.BlockSpec((None, d, tf), lambda i, j: (layer, 0, j)),
            pl.BlockSpec((None, tf, d), lambda i, j: (layer, j, 0)),
        ],
        out_specs=pl.BlockSpec((tm, d), lambda i, j: (i, 0)),
        scratch_shapes=[pltpu.VMEM((tm, d), BF16)],
        compiler_params=_params(("parallel", "arbitrary"), 48),
        name="swiglu_ffn",
    )(x, g, wg, wu, wd)


def _ple_kernel(x_ref, g_ref, wg_ref, p_ref, wp_ref, o_ref):
    x = x_ref[...]
    h = _rms(x, g_ref[...]).astype(BF16)
    gate = _sigmoid(jnp.dot(h, wg_ref[...], preferred_element_type=F32))
    emb = jnp.dot(p_ref[...].astype(BF16), wp_ref[...], preferred_element_type=F32)
    o_ref[...] = x + gate * emb


def _ple(x, g, wg, p, wp, *, layer, tm):
    t, d = x.shape
    pd = p.shape[2]
    return pl.pallas_call(
        _ple_kernel,
        out_shape=jax.ShapeDtypeStruct((t, d), F32),
        grid=(t // tm,),
        in_specs=[
            pl.BlockSpec((tm, d), lambda i: (i, 0)),
            _layer_vec(layer, d),
            pl.BlockSpec((None, d, d), lambda i: (layer, 0, 0)),
            pl.BlockSpec((None, tm, pd), lambda i: (layer, i, 0)),
            pl.BlockSpec((None, pd, d), lambda i: (layer, 0, 0)),
        ],
        out_specs=pl.BlockSpec((tm, d), lambda i: (i, 0)),
        compiler_params=_params(("parallel",), 56),
        name="gated_embedding",
    )(x, g, wg, p, wp)


def _final_norm_kernel(x_ref, g_ref, o_ref):
    o_ref[...] = _rms(x_ref[...], g_ref[...])


def _final_norm(x, g, *, row_off, rows, tm):
    d = x.shape[1]
    off = row_off // tm
    return pl.pallas_call(
        _final_norm_kernel,
        out_shape=jax.ShapeDtypeStruct((rows, d), F32),
        grid=(rows // tm,),
        in_specs=[pl.BlockSpec((tm, d), lambda i: (off + i, 0)), pl.BlockSpec((1, d), lambda i: (0, 0))],
        out_specs=pl.BlockSpec((tm, d), lambda i: (i, 0)),
        compiler_params=_params(("parallel",), 32),
        name="final_norm",
    )(x, g)


def _rope_tables(pos0, seq):
    half = DK_RET // 2
    inv = ROPE_BASE ** (-jnp.arange(half, dtype=F32) / half)
    ang = (pos0 + jnp.arange(seq, dtype=jnp.int32)).astype(F32)[:, None] * inv[None, :]
    return jnp.cos(ang), jnp.sin(ang)


def _decay_tables():
    log_g = jnp.log1p(-jnp.power(2.0, -5.0 - jnp.arange(H_RET, dtype=F32)))
    idx = jnp.arange(CHUNK, dtype=F32)
    intra = jnp.exp(log_g[:, None, None] * jnp.abs(idx[:, None] - idx[None, :]))
    q_dec = jnp.exp(log_g[:, None] * (idx[None, :] + 1.0))[..., None]
    k_dec = jnp.exp(log_g[:, None] * (CHUNK - 1.0 - idx[None, :]))[..., None]
    s_dec = jnp.exp(log_g * CHUNK)[:, None, None]
    return intra, q_dec, k_dec, s_dec


def _tile(n, want):
    t = min(n, want)
    while n % t:
        t //= 2
    return t


def kernel(x_prompt, x_sample, p_prompt, p_sample, state_ret, cache_fox_k, cache_fox_v, cache_fox_logf,
           state_pool, norm_mix, w_in, b_forget, w_out, w_pool, pool_scale, norm_ffn, w_gate, w_up, w_down,
           norm_ple, ple_gate, ple_proj, final_norm):
    bp, tp, d = x_prompt.shape
    bs, ts, _ = x_sample.shape
    depth = norm_mix.shape[0]
    past = cache_fox_k.shape[2]
    n_p, n_s = bp * tp, bs * ts
    n_tok = n_p + n_s
    ret_w = H_RET * DK_RET
    fox_w = H_FOX * DH_FOX
    main_w = 4 * ret_w + 3 * fox_w
    fox_base = (4 * ret_w) // fox_w

    x = jnp.concatenate([x_prompt.reshape(n_p, d), x_sample.reshape(n_s, d)], axis=0)
    p_all = jnp.concatenate([p_prompt.reshape(depth, n_p, -1), p_sample.reshape(depth, n_s, -1)], axis=1)
    tm = _tile(n_tok, 512)

    as_rows = lambda a: a.reshape(a.shape[0], 1, a.shape[1])
    norm_mix, norm_ffn, norm_ple, pool_scale = map(as_rows, (norm_mix, norm_ffn, norm_ple, pool_scale))
    w_in_b, w_out_b, w_pool_b = w_in.astype(BF16), w_out.astype(BF16), w_pool.astype(BF16)
    w_gate_b, w_up_b, w_down_b = w_gate.astype(BF16), w_up.astype(BF16), w_down.astype(BF16)
    ple_gate_b, ple_proj_b = ple_gate.astype(BF16), ple_proj.astype(BF16)
    w_f = jnp.pad(w_in[:, :, main_w:], ((0, 0), (0, 0), (0, LANES - H_FOX))).astype(BF16)
    b_f = jnp.pad(b_forget.astype(F32), ((0, 0), (0, LANES - H_FOX)))[:, None, :]
    cache_k = cache_fox_k.reshape(cache_fox_k.shape[:2] + (past * H_FOX, DH_FOX))
    cache_v = cache_fox_v.reshape(cache_fox_v.shape[:2] + (past * H_FOX, DH_FOX))

    decay = _decay_tables()
    rope_p = _rope_tables(0, tp)
    rope_s = _rope_tables(past, ts)
    zero_state = jnp.zeros((1, bp) + state_ret.shape[2:], F32)
    zero_buf = jnp.zeros((bp, POOL_HALO, d), F32)

    n_even = w_in.shape[0]
    kv_new = (None,) * 4
    ret_p = ret_s = None
    new_lf, new_pool_p, new_pool_s = [], [], []
    for i in range(depth):
        if i % 2 == 0:
            e = i // 2
            proj, *kv_new = _in_proj(x, norm_mix, w_in_b, kv_new, layer=i, wlayer=e, n_layers=n_even,
                                     n=main_w, n_p=n_p, k_col=4 * ret_w + fox_w,
                                     tm=_tile(math.gcd(n_p, n_s), 512))
            lf = _forget_gate(x, norm_mix, w_f, b_f, layer=i, wlayer=e, tm=tm)[:, :H_FOX]
            lf_p = lf[:n_p].reshape(bp, tp, H_FOX)
            lf_s = lf[n_p:].reshape(bs, ts, H_FOX)
            c_row_p = _cumsum_rows(jnp.transpose(lf_p, (0, 2, 1)))
            lf_all_s = jnp.concatenate([cache_fox_logf[e].astype(F32), lf_s], axis=1)
            c_row_s = _cumsum_rows(jnp.transpose(lf_all_s, (0, 2, 1)))
            c_col_p = jnp.transpose(c_row_p, (0, 2, 1))
            c_col_s = jnp.transpose(c_row_s[:, :, past:], (0, 2, 1))

            o_r, ret_p = _retention(proj, *rope_p, decay, zero_state, None, ret_p, layer=0, out_layer=e,
                                    n_layers=n_even, batch=bp, seq=tp, row_off=0,
                                    chunks=_tile(tp, 512) // CHUNK)
            o_r, ret_s = _retention(proj, *rope_s, decay, state_ret.astype(F32), o_r, ret_s, layer=e,
                                    out_layer=e, n_layers=n_even, batch=bs, seq=ts, row_off=n_p,
                                    chunks=ts // CHUNK)
            o_f = _fox_prompt(proj, c_col_p, c_row_p, batch=bp, seq=tp, row_off=0, tq=_tile(tp, 512),
                              col_base=fox_base)
            o_f = _fox_sample(proj, cache_k, cache_v, c_col_s, c_row_s, c_row_s[:, :, past:], o_f,
                              layer=e, batch=bs, seq=ts, row_off=n_p, tkc=_tile(past, 1024),
                              col_base=fox_base)
            x = _out_proj(o_r, o_f, w_out_b, x, layer=e, tm=tm)

            new_lf.append((lf_p, lf_s))
        else:
            o = i // 2
            buf_s = jnp.pad(state_pool[o].astype(F32), ((0, 0), (1, 0), (0, 0)))
            x_new, pool_p = _pool_mixer(x, zero_buf, norm_mix, w_pool_b, pool_scale, None, layer=i, mixer=o,
                                        batch=bp, seq=tp, row_off=0, tm=_tile(tp, 512), pos0=0)
            x, pool_s = _pool_mixer(x, buf_s, norm_mix, w_pool_b, pool_scale, x_new, layer=i, mixer=o,
                                    batch=bs, seq=ts, row_off=n_p, tm=ts, pos0=past)
            new_pool_p.append(pool_p[:, 1:])
            new_pool_s.append(pool_s[:, 1:])
        x = _ffn(x, norm_ffn, w_gate_b, w_up_b, w_down_b, layer=i, tm=tm, tf=512)
        x = _ple(x, norm_ple, ple_gate_b, p_all, ple_proj_b, layer=i, tm=_tile(n_tok, 256))

    g_final = final_norm[None, :]
    y_p = _final_norm(x, g_final, row_off=0, rows=n_p, tm=_tile(n_p, 512))
    y_s = _final_norm(x, g_final, row_off=n_p, rows=n_s, tm=_tile(math.gcd(n_p, n_s), 512))
    k_p, k_s, v_p, v_s = kv_new
    prompt_kv = (n_even, bp, tp, H_FOX, DH_FOX)
    sample_kv = (n_even, bs, ts, H_FOX, DH_FOX)
    return (
        y_p.reshape(bp, tp, d),
        y_s.reshape(bs, ts, d),
        ret_p,
        ret_s.astype(state_ret.dtype),
        k_p.reshape(prompt_kv),
        k_s.reshape(sample_kv),
        v_p.reshape(prompt_kv),
        v_s.reshape(sample_kv),
        jnp.stack([lf[0] for lf in new_lf]),
        jnp.stack([lf[1] for lf in new_lf]),
        jnp.stack(new_pool_p),
        jnp.stack(new_pool_s),
    )
```

```python
import functools
import math

import jax
import jax.numpy as jnp
from jax import lax
from jax.experimental import pallas as pl
from jax.experimental.pallas import tpu as pltpu

F32 = jnp.float32
BF16 = jnp.bfloat16

EPS = 1e-6
NEG_INF = -1e30
ROPE_BASE = 10000.0
LOG2E = math.log2(math.e)
CHUNK = 64
H_RET = 4
DK_RET = 256
H_FOX = 8
DH_FOX = 128
POOL_WINDOWS = (2, 4, 8, 16)
POOL_HALO = 16
LANES = 128
CUMSUM_CHUNK = 256

_NT = (((1,), (1,)), ((), ()))
_TN = (((0,), (0,)), ((), ()))


def _params(semantics, vmem_mib):
    return pltpu.CompilerParams(dimension_semantics=semantics, vmem_limit_bytes=vmem_mib << 20)


def _rms(x, g):
    return x * lax.rsqrt(jnp.mean(x * x, axis=-1, keepdims=True) + EPS) * g


def _sigmoid(x):
    return 1.0 / (1.0 + jnp.exp(-x))


def _layer_vec(layer, d):
    return pl.BlockSpec((None, 1, d), lambda *_: (layer, 0, 0))


def _call_into(kernel, dests, n_in, **kwargs):
    dests = {k: a for k, a in dests.items() if a is not None}
    if not dests:
        return pl.pallas_call(kernel, **kwargs)
    n_d = len(dests)

    def body(*refs):
        kernel(*refs[:n_in], *refs[n_in + n_d:])

    kwargs["in_specs"] = list(kwargs["in_specs"]) + [pl.BlockSpec(memory_space=pl.ANY)] * n_d
    aliases = {n_in + pos: out for pos, out in enumerate(dests)}
    call = pl.pallas_call(body, input_output_aliases=aliases, **kwargs)
    return lambda *args: call(*args, *dests.values())


def _in_proj_kernel(x_ref, g_ref, w_ref, o_ref, kp_ref, ks_ref, vp_ref, vs_ref, h_ref, *, tm, jk, ntp):
    i = pl.program_id(0)
    j = pl.program_id(1)

    @pl.when(j == 0)
    def _():
        h_ref[...] = _rms(x_ref[...], g_ref[...]).astype(h_ref.dtype)

    o_ref[...] = jnp.dot(h_ref[...], w_ref[...], preferred_element_type=F32)

    def to_head_rows(dst_ref):
        for h in range(H_FOX):
            dst_ref[pl.ds(h, tm, stride=H_FOX), :] = o_ref[:, h * DH_FOX:(h + 1) * DH_FOX]

    for jj, prompt_ref, sample_ref in ((jk, kp_ref, ks_ref), (jk + 1, vp_ref, vs_ref)):
        @pl.when((j == jj) & (i < ntp))
        def _():
            to_head_rows(prompt_ref)

        @pl.when((j == jj) & (i >= ntp))
        def _():
            to_head_rows(sample_ref)


def _norm_matmul_logsig_kernel(x_ref, g_ref, w_ref, b_ref, o_ref):
    h = _rms(x_ref[...], g_ref[...]).astype(BF16)
    z = jnp.dot(h, w_ref[...], preferred_element_type=F32) + b_ref[...]
    o_ref[...] = jnp.minimum(z, 0.0) - jnp.log1p(jnp.exp(-jnp.abs(z)))


def _in_proj(x, g, w, kv_dests, *, layer, wlayer, n_layers, n, n_p, k_col, tm):
    t, d = x.shape
    tn = H_FOX * DH_FOX
    ntp, nts = n_p // tm, (t - n_p) // tm
    rows = tm * H_FOX
    prompt_rows = pl.BlockSpec((rows, DH_FOX), lambda i, j: (wlayer * ntp + jnp.minimum(i, ntp - 1), 0))
    sample_rows = pl.BlockSpec((rows, DH_FOX),
                               lambda i, j: (wlayer * nts + jnp.clip(i - ntp, 0, nts - 1), 0))
    kv_p = jax.ShapeDtypeStruct((n_layers * n_p * H_FOX, DH_FOX), F32)
    kv_s = jax.ShapeDtypeStruct((n_layers * (t - n_p) * H_FOX, DH_FOX), F32)
    call = _call_into(
        functools.partial(_in_proj_kernel, tm=tm, jk=k_col // tn, ntp=ntp),
        {1 + n: a for n, a in enumerate(kv_dests)}, 3,
        out_shape=(jax.ShapeDtypeStruct((t, n), F32), kv_p, kv_s, kv_p, kv_s),
        grid=(t // tm, n // tn),
        in_specs=[
            pl.BlockSpec((tm, d), lambda i, j: (i, 0)),
            _layer_vec(layer, d),
            pl.BlockSpec((None, d, tn), lambda i, j: (wlayer, 0, j)),
        ],
        out_specs=(pl.BlockSpec((tm, tn), lambda i, j: (i, j)),
                   prompt_rows, sample_rows, prompt_rows, sample_rows),
        scratch_shapes=[pltpu.VMEM((tm, d), BF16)],
        compiler_params=_params(("arbitrary", "arbitrary"), 48),
        name="in_proj",
    )
    return call(x, g, w)


def _forget_gate(x, g, w, b, *, layer, wlayer, tm):
    t, d = x.shape
    n = w.shape[2]
    return pl.pallas_call(
        _norm_matmul_logsig_kernel,
        out_shape=jax.ShapeDtypeStruct((t, n), F32),
        grid=(t // tm,),
        in_specs=[
            pl.BlockSpec((tm, d), lambda i: (i, 0)),
            _layer_vec(layer, d),
            pl.BlockSpec((None, d, n), lambda i: (wlayer, 0, 0)),
            pl.BlockSpec((None, 1, n), lambda i: (wlayer, 0, 0)),
        ],
        out_specs=pl.BlockSpec((tm, n), lambda i: (i, 0)),
        compiler_params=_params(("parallel",), 32),
        name="forget_gate",
    )(x, g, w, b)


def _cumsum_kernel(lf_ref, c_ref, *, tk):
    r = lax.broadcasted_iota(jnp.int32, (CUMSUM_CHUNK, CUMSUM_CHUNK), 0)
    c = lax.broadcasted_iota(jnp.int32, (CUMSUM_CHUNK, CUMSUM_CHUNK), 1)
    upper = (r <= c).astype(F32)
    carry = jnp.zeros((H_FOX, 1), F32)
    for s in range(0, tk, CUMSUM_CHUNK):
        w = min(CUMSUM_CHUNK, tk - s)
        blk = lf_ref[0, :, s:s + w]
        cs = jnp.dot(blk, upper[:w, :w], precision=lax.Precision.HIGHEST,
                     preferred_element_type=F32) + carry
        c_ref[0, :, s:s + w] = cs
        carry = cs[:, w - 1:w]


def _cumsum_rows(lf_rows):
    b, h, tk = lf_rows.shape
    return pl.pallas_call(
        functools.partial(_cumsum_kernel, tk=tk),
        out_shape=jax.ShapeDtypeStruct((b, h, tk), F32),
        grid=(b,),
        in_specs=[pl.BlockSpec((1, h, tk), lambda i: (i, 0, 0))],
        out_specs=pl.BlockSpec((1, h, tk), lambda i: (i, 0, 0)),
        compiler_params=_params(("parallel",), 32),
        name="logf_cumsum",
    )(lf_rows)


def _rope(x, cos, sin):
    half = x.shape[-1] // 2
    x1, x2 = x[:, :half], x[:, half:]
    return jnp.concatenate([x1 * cos - x2 * sin, x1 * sin + x2 * cos], axis=-1)


def _retention_kernel(q_ref, k_ref, v_ref, g_ref, cos_ref, sin_ref, intra_ref, qdec_ref, kdec_ref,
                      sdec_ref, s0_ref, o_ref, s_out_ref, state_ref, *, chunks, heads, scale):
    i = pl.program_id(2)

    @pl.when(i == 0)
    def _():
        state_ref[...] = s0_ref[0]

    for c in range(chunks):
        rows = slice(c * CHUNK, (c + 1) * CHUNK)
        cos = cos_ref[rows, :]
        sin = sin_ref[rows, :]
        for h in range(heads):
            cols = slice(h * DK_RET, (h + 1) * DK_RET)
            q = _rope(q_ref[rows, cols], cos, sin)
            k = _rope(k_ref[rows, cols], cos, sin) * scale
            qb = q.astype(BF16)
            vb = v_ref[rows, cols].astype(BF16)
            state = state_ref[h]
            sc = lax.dot_general(qb, k.astype(BF16), _NT, preferred_element_type=F32) * intra_ref[h]
            o = (jnp.dot(sc.astype(BF16), vb, preferred_element_type=F32)
                 + jnp.dot(qb, state.astype(BF16), preferred_element_type=F32) * qdec_ref[h])
            kd = (k * kdec_ref[h]).astype(BF16)
            state_ref[h] = state * sdec_ref[h] + lax.dot_general(kd, vb, _TN, preferred_element_type=F32)
            o = o * lax.rsqrt(jnp.mean(o * o, axis=-1, keepdims=True) + EPS)
            gate = g_ref[rows, cols]
            o_ref[rows, cols] = (o * (gate * _sigmoid(gate))).astype(o_ref.dtype)

    @pl.when(i == pl.num_programs(2) - 1)
    def _():
        s_out_ref[0] = state_ref[...]


def _retention(proj, cos, sin, decay, state, dest, state_dest, *, layer, out_layer, n_layers, batch, seq,
               row_off, chunks, heads):
    tblk = chunks * CHUNK
    nblk = seq // tblk
    off = row_off // tblk
    intra, q_dec, k_dec, s_dec = decay

    def col(base):
        return pl.BlockSpec((tblk, heads * DK_RET),
                            lambda b, h, i: (off + b * nblk + i, base // heads + h))

    per_head = lambda shape: pl.BlockSpec((heads,) + shape, lambda b, h, i: (h, 0, 0))
    table = pl.BlockSpec((tblk, DK_RET // 2), lambda b, h, i: (i, 0))
    state_block = (None, 1, heads, DK_RET, DK_RET)
    call = _call_into(
        functools.partial(_retention_kernel, chunks=chunks, heads=heads, scale=DK_RET ** -0.5),
        {0: dest, 1: state_dest}, 11,
        out_shape=(jax.ShapeDtypeStruct((proj.shape[0], H_RET * DK_RET), BF16),
                   jax.ShapeDtypeStruct((n_layers, batch, H_RET, DK_RET, DK_RET), F32)),
        grid=(batch, H_RET // heads, nblk),
        in_specs=[col(0), col(H_RET), col(2 * H_RET), col(3 * H_RET), table, table,
                  per_head((CHUNK, CHUNK)), per_head((CHUNK, 1)), per_head((CHUNK, 1)),
                  per_head((1, 1)),
                  pl.BlockSpec(state_block, lambda b, h, i: (layer, b, h, 0, 0))],
        out_specs=(col(0), pl.BlockSpec(state_block, lambda b, h, i: (out_layer, b, h, 0, 0))),
        scratch_shapes=[pltpu.VMEM((heads, DK_RET, DK_RET), F32)],
        compiler_params=_params(("parallel", "parallel", "arbitrary"), 32),
        name="retention",
    )
    return call(proj, proj, proj, proj, cos, sin, intra, q_dec, k_dec, s_dec, state)


def _fox_prompt_kernel(qi_ref, kj_ref, q_ref, k_ref, v_ref, cq_ref, ck_ref, o_ref, m_ref, l_ref, acc_ref, *,
                       tq, scale):
    qi = qi_ref[pl.program_id(1)]
    kj = kj_ref[pl.program_id(1)]

    @pl.when(kj == 0)
    def _():
        m_ref[...] = jnp.full(m_ref.shape, NEG_INF, F32)
        l_ref[...] = jnp.zeros(l_ref.shape, F32)
        acc_ref[...] = jnp.zeros(acc_ref.shape, F32)

    def step(masked):
        if masked:
            key = lax.broadcasted_iota(jnp.int32, (tq, tq), 0)
            qry = lax.broadcasted_iota(jnp.int32, (tq, tq), 1)
            visible = key <= qry
        for h in range(H_FOX):
            cols = slice(h * DH_FOX, (h + 1) * DH_FOX)
            q = (q_ref[:, cols] * (scale * LOG2E)).astype(BF16)
            k = k_ref[:, cols].astype(BF16)
            bias = cq_ref[0, h:h + 1, :] * LOG2E - ck_ref[0, :, h:h + 1] * LOG2E
            s = lax.dot_general(k, q, _NT, preferred_element_type=F32) + bias
            if masked:
                s = jnp.where(visible, s, NEG_INF)
            m_prev = m_ref[h]
            m_new = jnp.maximum(m_prev, jnp.max(s, axis=0, keepdims=True))
            p = jnp.exp2(s - m_new)
            alpha = jnp.exp2(m_prev - m_new)
            l_ref[h] = alpha * l_ref[h] + jnp.sum(p, axis=0, keepdims=True)
            pv = lax.dot_general(v_ref[:, cols].astype(BF16), p.astype(BF16), _TN,
                                 preferred_element_type=F32)
            acc_ref[h] = alpha * acc_ref[h] + pv
            m_ref[h] = m_new

    @pl.when(kj < qi)
    def _():
        step(False)

    @pl.when(kj == qi)
    def _():
        step(True)
        for h in range(H_FOX):
            o = (acc_ref[h] / l_ref[h]).T
            o_ref[:, h * DH_FOX:(h + 1) * DH_FOX] = o.astype(o_ref.dtype)


def _fox_prompt(proj, c_col, c_row, *, batch, seq, row_off, tq, col_base):
    nq = seq // tq
    off = row_off // tq
    width = H_FOX * DH_FOX
    qb, kb, vb = (col_base + n for n in range(3))

    pairs = [(i, j) for i in range(nq) for j in range(i + 1)]
    qi_tab = jnp.asarray([i for i, _ in pairs], jnp.int32)
    kj_tab = jnp.asarray([j for _, j in pairs], jnp.int32)
    grid_spec = pltpu.PrefetchScalarGridSpec(
        num_scalar_prefetch=2,
        grid=(batch, len(pairs)),
        in_specs=[
            pl.BlockSpec((tq, width), lambda b, s, qi, kj: (off + b * nq + qi[s], qb)),
            pl.BlockSpec((tq, width), lambda b, s, qi, kj: (off + b * nq + kj[s], kb)),
            pl.BlockSpec((tq, width), lambda b, s, qi, kj: (off + b * nq + kj[s], vb)),
            pl.BlockSpec((1, H_FOX, tq), lambda b, s, qi, kj: (b, 0, qi[s])),
            pl.BlockSpec((1, tq, H_FOX), lambda b, s, qi, kj: (b, kj[s], 0)),
        ],
        out_specs=pl.BlockSpec((tq, width), lambda b, s, qi, kj: (off + b * nq + qi[s], 0)),
        scratch_shapes=[pltpu.VMEM((H_FOX, 1, tq), F32), pltpu.VMEM((H_FOX, 1, tq), F32),
                        pltpu.VMEM((H_FOX, DH_FOX, tq), F32)],
    )
    return pl.pallas_call(
        functools.partial(_fox_prompt_kernel, tq=tq, scale=DH_FOX ** -0.5),
        out_shape=jax.ShapeDtypeStruct((proj.shape[0], width), BF16),
        grid_spec=grid_spec,
        compiler_params=_params(("parallel", "arbitrary"), 48),
        name="fox_prompt",
    )(qi_tab, kj_tab, proj, proj, proj, c_row, c_col)


def _fox_sample_kernel(q_ref, kc_ref, vc_ref, kn_ref, vn_ref, cq_ref, ckc_ref, ckn_ref, o_ref,
                       m_ref, l_ref, acc_ref, s_ref, p_ref, *, tq, tkc, scale):
    j = pl.program_id(1)
    last = pl.num_programs(1) - 1

    @pl.when(j == 0)
    def _():
        m_ref[...] = jnp.full(m_ref.shape, NEG_INF, F32)
        l_ref[...] = jnp.zeros(l_ref.shape, F32)
        acc_ref[...] = jnp.zeros(acc_ref.shape, F32)

    cq = cq_ref[0] * LOG2E

    def scores(h, k, ck, visible=None):
        q = (q_ref[:, h * DH_FOX:(h + 1) * DH_FOX] * (scale * LOG2E)).astype(BF16)
        s = (lax.dot_general(q, k, _NT, preferred_element_type=F32)
             + (cq[:, h:h + 1] - ck[h:h + 1, :]))
        if visible is not None:
            s = jnp.where(visible, s, NEG_INF)
        s_ref[h * tq:(h + 1) * tq, :s.shape[1]] = s

    def softmax_update(width):
        s = s_ref[:, :width]
        m_prev = m_ref[...]
        m_new = jnp.maximum(m_prev, jnp.max(s, axis=-1, keepdims=True))
        p = jnp.exp2(s - m_new)
        alpha = jnp.exp2(m_prev - m_new)
        l_ref[...] = alpha * l_ref[...] + jnp.sum(p, axis=-1, keepdims=True)
        m_ref[...] = m_new
        p_ref[:, :width] = p.astype(BF16)
        return alpha

    def values(h, v, alpha, width):
        rows = slice(h * tq, (h + 1) * tq)
        acc_ref[h] = alpha[rows] * acc_ref[h] + jnp.dot(p_ref[rows, :width], v, preferred_element_type=F32)

    ck = ckc_ref[0] * LOG2E
    for h in range(H_FOX):
        scores(h, kc_ref[pl.ds(h, tkc, stride=H_FOX), :].astype(BF16), ck)
    alpha = softmax_update(tkc)
    for h in range(H_FOX):
        values(h, vc_ref[pl.ds(h, tkc, stride=H_FOX), :].astype(BF16), alpha, tkc)

    @pl.when(j == last)
    def _():
        row = lax.broadcasted_iota(jnp.int32, (tq, tq), 0)
        col = lax.broadcasted_iota(jnp.int32, (tq, tq), 1)
        visible = col <= row
        ck = ckn_ref[0] * LOG2E
        for h in range(H_FOX):
            scores(h, kn_ref[:, h * DH_FOX:(h + 1) * DH_FOX].astype(BF16), ck, visible)
        alpha = softmax_update(tq)
        for h in range(H_FOX):
            cols = slice(h * DH_FOX, (h + 1) * DH_FOX)
            values(h, vn_ref[:, cols].astype(BF16), alpha, tq)
            o_ref[:, cols] = (acc_ref[h] / l_ref[h * tq:(h + 1) * tq]).astype(o_ref.dtype)


def _fox_sample(proj, cache_k, cache_v, c_col, c_row_cache, c_row_new, dest, *, layer, batch, seq,
                row_off, tkc, col_base):
    past = cache_k.shape[2] // H_FOX
    nkc = past // tkc
    off = row_off // seq
    width = H_FOX * DH_FOX
    qb, kb, vb = (col_base + n for n in range(3))
    cache_spec = pl.BlockSpec((None, None, tkc * H_FOX, DH_FOX), lambda b, j: (layer, b, j, 0))

    call = _call_into(
        functools.partial(_fox_sample_kernel, tq=seq, tkc=tkc, scale=DH_FOX ** -0.5), {0: dest}, 8,
        out_shape=jax.ShapeDtypeStruct((proj.shape[0], width), BF16),
        grid=(batch, nkc),
        in_specs=[
            pl.BlockSpec((seq, width), lambda b, j: (off + b, qb)),
            cache_spec,
            cache_spec,
            pl.BlockSpec((seq, width), lambda b, j: (off + b, kb)),
            pl.BlockSpec((seq, width), lambda b, j: (off + b, vb)),
            pl.BlockSpec((1, seq, H_FOX), lambda b, j: (b, 0, 0)),
            pl.BlockSpec((1, H_FOX, tkc), lambda b, j: (b, 0, j)),
            pl.BlockSpec((1, H_FOX, seq), lambda b, j: (b, 0, 0)),
        ],
        out_specs=pl.BlockSpec((seq, width), lambda b, j: (off + b, 0)),
        scratch_shapes=[pltpu.VMEM((H_FOX * seq, 1), F32), pltpu.VMEM((H_FOX * seq, 1), F32),
                        pltpu.VMEM((H_FOX, seq, DH_FOX), F32),
                        pltpu.VMEM((H_FOX * seq, tkc), F32), pltpu.VMEM((H_FOX * seq, tkc), BF16)],
        compiler_params=_params(("parallel", "arbitrary"), 48),
        name="fox_sample",
    )
    return call(proj, cache_k, cache_v, proj, proj, c_col, c_row_cache, c_row_new)


def _out_proj_kernel(a_ref, b_ref, w_ref, x_ref, o_ref):
    ka = a_ref.shape[1]
    o_ref[...] = (x_ref[...]
                  + jnp.dot(a_ref[...], w_ref[:ka, :], preferred_element_type=F32)
                  + jnp.dot(b_ref[...], w_ref[ka:, :], preferred_element_type=F32))


def _out_proj(a, b, w, x, *, layer, tm):
    t, d = x.shape
    ka, kb = a.shape[1], b.shape[1]
    return pl.pallas_call(
        _out_proj_kernel,
        out_shape=jax.ShapeDtypeStruct((t, d), F32),
        grid=(t // tm,),
        in_specs=[
            pl.BlockSpec((tm, ka), lambda i: (i, 0)),
            pl.BlockSpec((tm, kb), lambda i: (i, 0)),
            pl.BlockSpec((None, ka + kb, d), lambda i: (layer, 0, 0)),
            pl.BlockSpec((tm, d), lambda i: (i, 0)),
        ],
        out_specs=pl.BlockSpec((tm, d), lambda i: (i, 0)),
        compiler_params=_params(("parallel",), 48),
        name="out_proj",
    )(a, b, w, x)


def _pool_kernel(x_ref, halo_ref, buf_ref, g_ref, w_ref, ps_ref, o_ref, bo_ref, full_ref, *, tm, pos0):
    i = pl.program_id(1)
    g = g_ref[...]
    x = x_ref[...]
    u = _rms(x, g)
    hist = jnp.where(i == 0, buf_ref[0], _rms(halo_ref[...], g))
    full_ref[0:POOL_HALO, :] = hist
    full_ref[POOL_HALO:POOL_HALO + tm, :] = u
    pos = pos0 + i * tm + lax.broadcasted_iota(jnp.int32, (tm, 1), 0)
    gc = x.shape[1] // len(POOL_WINDOWS)
    for n, win in enumerate(POOL_WINDOWS):
        cols = slice(n * gc, (n + 1) * gc)
        total = u[:, cols]
        for back in range(1, win):
            total = total + full_ref[POOL_HALO - back:POOL_HALO - back + tm, cols]
        inv_cnt = 1.0 / jnp.minimum(win, pos + 1).astype(F32)
        diff = (total * inv_cnt - u[:, cols]).astype(BF16)
        y = jnp.dot(diff, w_ref[n], preferred_element_type=F32) * ps_ref[:, cols]
        o_ref[:, cols] = x[:, cols] + y
    bo_ref[0] = u[tm - POOL_HALO:, :]


def _pool_mixer(x, buf, g, w, ps, dest, *, layer, mixer, batch, seq, row_off, tm, pos0):
    d = x.shape[1]
    nt = seq // tm
    off = row_off // tm
    per_halo = tm // POOL_HALO
    halo_off = row_off // POOL_HALO

    def halo_map(b, i):
        return (jnp.maximum(halo_off + (b * nt + i) * per_halo - 1, 0), 0)

    rows = pl.BlockSpec((tm, d), lambda b, i: (off + b * nt + i, 0))
    call = _call_into(
        functools.partial(_pool_kernel, tm=tm, pos0=pos0), {0: dest}, 6,
        out_shape=(jax.ShapeDtypeStruct(x.shape, F32),
                   jax.ShapeDtypeStruct((batch, POOL_HALO, d), F32)),
        grid=(batch, nt),
        in_specs=[
            rows,
            pl.BlockSpec((POOL_HALO, d), halo_map),
            pl.BlockSpec((1, POOL_HALO, d), lambda b, i: (b, 0, 0)),
            _layer_vec(layer, d),
            pl.BlockSpec((None,) + w.shape[1:], lambda b, i: (mixer, 0, 0, 0)),
            _layer_vec(mixer, d),
        ],
        out_specs=(rows, pl.BlockSpec((1, POOL_HALO, d), lambda b, i: (b, 0, 0))),
        scratch_shapes=[pltpu.VMEM((POOL_HALO + tm, d), F32)],
        compiler_params=_params(("parallel", "arbitrary"), 48),
        name="pool_mixer",
    )
    return call(x, x, buf, g, w, ps)


def _ffn_kernel(x_ref, g_ref, wg_ref, wu_ref, wd_ref, o_ref, h_ref):
    @pl.when(pl.program_id(1) == 0)
    def _():
        x = x_ref[...]
        h_ref[...] = _rms(x, g_ref[...]).astype(h_ref.dtype)
        o_ref[...] = x

    h = h_ref[...]
    a = jnp.dot(h, wg_ref[...].astype(BF16), preferred_element_type=F32)
    u = jnp.dot(h, wu_ref[...].astype(BF16), preferred_element_type=F32)
    act = (a * _sigmoid(a) * u).astype(BF16)
    o_ref[...] += jnp.dot(act, wd_ref[...].astype(BF16), preferred_element_type=F32)


def _ffn(x, g, wg, wu, wd, *, layer, tm, tf):
    t, d = x.shape
    f = wg.shape[2]
    return pl.pallas_call(
        _ffn_kernel,
        out_shape=jax.ShapeDtypeStruct((t, d), F32),
        grid=(t // tm, f // tf),
        in_specs=[
            pl.BlockSpec((tm, d), lambda i, j: (i, 0)),
            _layer_vec(layer, d),
            pl.BlockSpec((None, d, tf), lambda i, j: (layer, 0, j)),
            pl.BlockSpec((None, d, tf), lambda i, j: (layer, 0, j)),
            pl.BlockSpec((None, tf, d), lambda i, j: (layer, j, 0)),
        ],
        out_specs=pl.BlockSpec((tm, d), lambda i, j: (i, 0)),
        scratch_shapes=[pltpu.VMEM((tm, d), BF16)],
        compiler_params=_params(("parallel", "arbitrary"), 56),
        name="swiglu_ffn",
    )(x, g, wg, wu, wd)


def _ple_kernel(x_ref, g_ref, wg_ref, p_ref, wp_ref, o_ref):
    x = x_ref[...]
    h = _rms(x, g_ref[...]).astype(BF16)
    gate = _sigmoid(jnp.dot(h, wg_ref[...], preferred_element_type=F32))
    emb = jnp.dot(p_ref[...].astype(BF16), wp_ref[...], preferred_element_type=F32)
    o_ref[...] = x + gate * emb


def _ple(x, g, wg, p, wp, *, layer, tm):
    t, d = x.shape
    pd = p.shape[2]
    return pl.pallas_call(
        _ple_kernel,
        out_shape=jax.ShapeDtypeStruct((t, d), F32),
        grid=(t // tm,),
        in_specs=[
            pl.BlockSpec((tm, d), lambda i: (i, 0)),
            _layer_vec(layer, d),
            pl.BlockSpec((None, d, d), lambda i: (layer, 0, 0)),
            pl.BlockSpec((None, tm, pd), lambda i: (layer, i, 0)),
            pl.BlockSpec((None, pd, d), lambda i: (layer, 0, 0)),
        ],
        out_specs=pl.BlockSpec((tm, d), lambda i: (i, 0)),
        compiler_params=_params(("parallel",), 56),
        name="gated_embedding",
    )(x, g, wg, p, wp)


def _final_norm_kernel(x_ref, g_ref, o_ref):
    o_ref[...] = _rms(x_ref[...], g_ref[...])


def _final_norm(x, g, *, row_off, rows, tm):
    d = x.shape[1]
    off = row_off // tm
    return pl.pallas_call(
        _final_norm_kernel,
        out_shape=jax.ShapeDtypeStruct((rows, d), F32),
        grid=(rows // tm,),
        in_specs=[pl.BlockSpec((tm, d), lambda i: (off + i, 0)), pl.BlockSpec((1, d), lambda i: (0, 0))],
        out_specs=pl.BlockSpec((tm, d), lambda i: (i, 0)),
        compiler_params=_params(("parallel",), 32),
        name="final_norm",
    )(x, g)


def _rope_tables(pos0, seq):
    half = DK_RET // 2
    inv = ROPE_BASE ** (-jnp.arange(half, dtype=F32) / half)
    ang = (pos0 + jnp.arange(seq, dtype=jnp.int32)).astype(F32)[:, None] * inv[None, :]
    return jnp.cos(ang), jnp.sin(ang)


def _decay_tables():
    log_g = jnp.log1p(-jnp.power(2.0, -5.0 - jnp.arange(H_RET, dtype=F32)))
    idx = jnp.arange(CHUNK, dtype=F32)
    intra = jnp.exp(log_g[:, None, None] * jnp.abs(idx[:, None] - idx[None, :]))
    q_dec = jnp.exp(log_g[:, None] * (idx[None, :] + 1.0))[..., None]
    k_dec = jnp.exp(log_g[:, None] * (CHUNK - 1.0 - idx[None, :]))[..., None]
    s_dec = jnp.exp(log_g * CHUNK)[:, None, None]
    return intra, q_dec, k_dec, s_dec


def _tile(n, want):
    t = min(n, want)
    while n % t:
        t //= 2
    return t


def kernel(x_prompt, x_sample, p_prompt, p_sample, state_ret, cache_fox_k, cache_fox_v, cache_fox_logf,
           state_pool, norm_mix, w_in, b_forget, w_out, w_pool, pool_scale, norm_ffn, w_gate, w_up, w_down,
           norm_ple, ple_gate, ple_proj, final_norm):
    bp, tp, d = x_prompt.shape
    bs, ts, _ = x_sample.shape
    depth = norm_mix.shape[0]
    past = cache_fox_k.shape[2]
    n_p, n_s = bp * tp, bs * ts
    n_tok = n_p + n_s
    ret_w = H_RET * DK_RET
    fox_w = H_FOX * DH_FOX
    main_w = 4 * ret_w + 3 * fox_w
    fox_base = (4 * ret_w) // fox_w

    x = jnp.concatenate([x_prompt.reshape(n_p, d), x_sample.reshape(n_s, d)], axis=0)
    p_all = jnp.concatenate([p_prompt.reshape(depth, n_p, -1), p_sample.reshape(depth, n_s, -1)], axis=1)
    tm = _tile(n_tok, 512)

    as_rows = lambda a: a.reshape(a.shape[0], 1, a.shape[1])
    norm_mix, norm_ffn, norm_ple, pool_scale = map(as_rows, (norm_mix, norm_ffn, norm_ple, pool_scale))
    w_in_b, w_out_b, w_pool_b = w_in.astype(BF16), w_out.astype(BF16), w_pool.astype(BF16)
    ple_gate_b, ple_proj_b = ple_gate.astype(BF16), ple_proj.astype(BF16)
    w_f = jnp.pad(w_in[:, :, main_w:], ((0, 0), (0, 0), (0, LANES - H_FOX))).astype(BF16)
    b_f = jnp.pad(b_forget.astype(F32), ((0, 0), (0, LANES - H_FOX)))[:, None, :]
    cache_k = cache_fox_k.reshape(cache_fox_k.shape[:2] + (past * H_FOX, DH_FOX))
    cache_v = cache_fox_v.reshape(cache_fox_v.shape[:2] + (past * H_FOX, DH_FOX))

    decay = _decay_tables()
    rope_p = _rope_tables(0, tp)
    rope_s = _rope_tables(past, ts)
    zero_state = jnp.zeros((1, bp) + state_ret.shape[2:], F32)
    zero_buf = jnp.zeros((bp, POOL_HALO, d), F32)

    n_even = w_in.shape[0]
    kv_new = (None,) * 4
    ret_p = ret_s = None
    new_lf, new_pool_p, new_pool_s = [], [], []
    for i in range(depth):
        if i % 2 == 0:
            e = i // 2
            proj, *kv_new = _in_proj(x, norm_mix, w_in_b, kv_new, layer=i, wlayer=e, n_layers=n_even,
                                     n=main_w, n_p=n_p, k_col=4 * ret_w + fox_w,
                                     tm=_tile(math.gcd(n_p, n_s), 512))
            lf = _forget_gate(x, norm_mix, w_f, b_f, layer=i, wlayer=e, tm=tm)[:, :H_FOX]
            lf_p = lf[:n_p].reshape(bp, tp, H_FOX)
            lf_s = lf[n_p:].reshape(bs, ts, H_FOX)
            c_row_p = _cumsum_rows(jnp.transpose(lf_p, (0, 2, 1)))
            lf_all_s = jnp.concatenate([cache_fox_logf[e].astype(F32), lf_s], axis=1)
            c_row_s = _cumsum_rows(jnp.transpose(lf_all_s, (0, 2, 1)))
            c_col_p = jnp.transpose(c_row_p, (0, 2, 1))
            c_col_s = jnp.transpose(c_row_s[:, :, past:], (0, 2, 1))

            o_r, ret_p = _retention(proj, *rope_p, decay, zero_state, None, ret_p, layer=0, out_layer=e,
                                    n_layers=n_even, batch=bp, seq=tp, row_off=0,
                                    chunks=_tile(tp, 512) // CHUNK, heads=2)
            o_r, ret_s = _retention(proj, *rope_s, decay, state_ret.astype(F32), o_r, ret_s, layer=e,
                                    out_layer=e, n_layers=n_even, batch=bs, seq=ts, row_off=n_p,
                                    chunks=ts // CHUNK, heads=H_RET)
            o_f = _fox_prompt(proj, c_col_p, c_row_p, batch=bp, seq=tp, row_off=0, tq=_tile(tp, 512),
                              col_base=fox_base)
            o_f = _fox_sample(proj, cache_k, cache_v, c_col_s, c_row_s, c_row_s[:, :, past:], o_f,
                              layer=e, batch=bs, seq=ts, row_off=n_p, tkc=_tile(past, 1024),
                              col_base=fox_base)
            x = _out_proj(o_r, o_f, w_out_b, x, layer=e, tm=tm)

            new_lf.append((lf_p, lf_s))
        else:
            o = i // 2
            buf_s = jnp.pad(state_pool[o].astype(F32), ((0, 0), (1, 0), (0, 0)))
            x_new, pool_p = _pool_mixer(x, zero_buf, norm_mix, w_pool_b, pool_scale, None, layer=i, mixer=o,
                                        batch=bp, seq=tp, row_off=0, tm=_tile(tp, 512), pos0=0)
            x, pool_s = _pool_mixer(x, buf_s, norm_mix, w_pool_b, pool_scale, x_new, layer=i, mixer=o,
                                    batch=bs, seq=ts, row_off=n_p, tm=ts, pos0=past)
            new_pool_p.append(pool_p[:, 1:])
            new_pool_s.append(pool_s[:, 1:])
        x = _ffn(x, norm_ffn, w_gate, w_up, w_down, layer=i, tm=_tile(n_tok, 1024), tf=256)
        x = _ple(x, norm_ple, ple_gate_b, p_all, ple_proj_b, layer=i, tm=_tile(n_tok, 256))

    g_final = final_norm[None, :]
    y_p = _final_norm(x, g_final, row_off=0, rows=n_p, tm=_tile(n_p, 512))
    y_s = _final_norm(x, g_final, row_off=n_p, rows=n_s, tm=_tile(math.gcd(n_p, n_s), 512))
    k_p, k_s, v_p, v_s = kv_new
    prompt_kv = (n_even, bp, tp, H_FOX, DH_FOX)
    sample_kv = (n_even, bs, ts, H_FOX, DH_FOX)
    return (
        y_p.reshape(bp, tp, d),
        y_s.reshape(bs, ts, d),
        ret_p,
        ret_s.astype(state_ret.dtype),
        k_p.reshape(prompt_kv),
        k_s.reshape(sample_kv),
        v_p.reshape(prompt_kv),
        v_s.reshape(sample_kv),
        jnp.stack([lf[0] for lf in new_lf]),
        jnp.stack([lf[1] for lf in new_lf]),
        jnp.stack(new_pool_p),
        jnp.stack(new_pool_s),
    )
```

```python
import functools
import math

import jax
import jax.numpy as jnp
from jax import lax
from jax.experimental import pallas as pl
from jax.experimental.pallas import tpu as pltpu

F32 = jnp.float32
BF16 = jnp.bfloat16

EPS = 1e-6
NEG_INF = -1e30
ROPE_BASE = 10000.0
LOG2E = math.log2(math.e)
CHUNK = 64
H_RET = 4
DK_RET = 256
H_FOX = 8
DH_FOX = 128
POOL_WINDOWS = (2, 4, 8, 16)
POOL_HALO = 16
LANES = 128
CUMSUM_CHUNK = 256

_NT = (((1,), (1,)), ((), ()))
_TN = (((0,), (0,)), ((), ()))


def _params(semantics, vmem_mib):
    return pltpu.CompilerParams(dimension_semantics=semantics, vmem_limit_bytes=vmem_mib << 20)


def _rms(x, g):
    return x * lax.rsqrt(jnp.mean(x * x, axis=-1, keepdims=True) + EPS) * g


def _sigmoid(x):
    return 1.0 / (1.0 + jnp.exp(-x))


def _layer_vec(layer, d):
    return pl.BlockSpec((None, 1, d), lambda *_: (layer, 0, 0))


def _call_into(kernel, dests, n_in, **kwargs):
    dests = {k: a for k, a in dests.items() if a is not None}
    if not dests:
        return pl.pallas_call(kernel, **kwargs)
    n_d = len(dests)

    def body(*refs):
        kernel(*refs[:n_in], *refs[n_in + n_d:])

    kwargs["in_specs"] = list(kwargs["in_specs"]) + [pl.BlockSpec(memory_space=pl.ANY)] * n_d
    aliases = {n_in + pos: out for pos, out in enumerate(dests)}
    call = pl.pallas_call(body, input_output_aliases=aliases, **kwargs)
    return lambda *args: call(*args, *dests.values())


def _in_proj_kernel(x_ref, g_ref, w_ref, wf_ref, bf_ref, o_ref, lf_ref, kp_ref, ks_ref, vp_ref, vs_ref,
                    h_ref, *, th, jk, ntp):
    i = pl.program_id(0)
    j = pl.program_id(1)
    half = pl.program_id(2)

    @pl.when((j == 0) & (half == 0))
    def _():
        h_ref[...] = _rms(x_ref[...], g_ref[...]).astype(h_ref.dtype)

    h = h_ref[pl.ds(pl.multiple_of(half * th, th), th), :]
    o_ref[...] = jnp.dot(h, w_ref[...], preferred_element_type=F32)

    @pl.when(j == 0)
    def _():
        z = jnp.dot(h, wf_ref[...], preferred_element_type=F32) + bf_ref[...]
        lf_ref[...] = jnp.minimum(z, 0.0) - jnp.log1p(jnp.exp(-jnp.abs(z)))

    def to_head_rows(dst_ref):
        for hd in range(H_FOX):
            dst_ref[pl.ds(hd, th, stride=H_FOX), :] = o_ref[:, hd * DH_FOX:(hd + 1) * DH_FOX]

    for jj, prompt_ref, sample_ref in ((jk, kp_ref, ks_ref), (jk + 1, vp_ref, vs_ref)):
        @pl.when((j == jj) & (i < ntp))
        def _():
            to_head_rows(prompt_ref)

        @pl.when((j == jj) & (i >= ntp))
        def _():
            to_head_rows(sample_ref)


def _in_proj(x, g, w, wf, bf, kv_dests, *, layer, wlayer, n_layers, n, n_p, k_col, tm):
    t, d = x.shape
    n_s = t - n_p
    tn = H_FOX * DH_FOX
    th = tm // 2
    ntp, nts = n_p // tm, n_s // tm
    jk = k_col // tn

    def half_tile(at_j):
        return lambda i, j, half: 2 * i + jnp.where(j > at_j, 1, jnp.where(j == at_j, half, 0))

    def head_rows(at_j, n_tiles, first):
        sub = half_tile(at_j)
        return pl.BlockSpec((th * H_FOX, DH_FOX),
                            lambda i, j, half: (wlayer * n_tiles
                                                + jnp.clip(sub(i, j, half) - first, 0, n_tiles - 1), 0))

    kv_p = jax.ShapeDtypeStruct((n_layers * n_p * H_FOX, DH_FOX), F32)
    kv_s = jax.ShapeDtypeStruct((n_layers * n_s * H_FOX, DH_FOX), F32)
    lanes = wf.shape[2]
    lf_sub = half_tile(0)
    call = _call_into(
        functools.partial(_in_proj_kernel, th=th, jk=jk, ntp=ntp),
        {2 + k: a for k, a in enumerate(kv_dests)}, 5,
        out_shape=(jax.ShapeDtypeStruct((t, n), F32), jax.ShapeDtypeStruct((t, lanes), F32),
                   kv_p, kv_s, kv_p, kv_s),
        grid=(t // tm, n // tn, 2),
        in_specs=[
            pl.BlockSpec((tm, d), lambda i, j, half: (i, 0)),
            _layer_vec(layer, d),
            pl.BlockSpec((None, d, tn), lambda i, j, half: (wlayer, 0, j)),
            pl.BlockSpec((None, d, lanes), lambda i, j, half: (wlayer, 0, 0)),
            pl.BlockSpec((None, 1, lanes), lambda i, j, half: (wlayer, 0, 0)),
        ],
        out_specs=(pl.BlockSpec((th, tn), lambda i, j, half: (2 * i + half, j)),
                   pl.BlockSpec((th, lanes), lambda i, j, half: (lf_sub(i, j, half), 0)),
                   head_rows(jk, 2 * ntp, 0), head_rows(jk, 2 * nts, 2 * ntp),
                   head_rows(jk + 1, 2 * ntp, 0), head_rows(jk + 1, 2 * nts, 2 * ntp)),
        scratch_shapes=[pltpu.VMEM((tm, d), BF16)],
        compiler_params=_params(("arbitrary", "arbitrary", "arbitrary"), 56),
        name="in_proj",
    )
    return call(x, g, w, wf, bf)


def _cumsum_kernel(lf_ref, c_ref, *, tk):
    r = lax.broadcasted_iota(jnp.int32, (CUMSUM_CHUNK, CUMSUM_CHUNK), 0)
    c = lax.broadcasted_iota(jnp.int32, (CUMSUM_CHUNK, CUMSUM_CHUNK), 1)
    upper = (r <= c).astype(F32)
    carry = jnp.zeros((H_FOX, 1), F32)
    for s in range(0, tk, CUMSUM_CHUNK):
        w = min(CUMSUM_CHUNK, tk - s)
        blk = lf_ref[0, :, s:s + w]
        cs = jnp.dot(blk, upper[:w, :w], precision=lax.Precision.HIGHEST,
                     preferred_element_type=F32) + carry
        c_ref[0, :, s:s + w] = cs
        carry = cs[:, w - 1:w]


def _cumsum_rows(lf_rows):
    b, h, tk = lf_rows.shape
    return pl.pallas_call(
        functools.partial(_cumsum_kernel, tk=tk),
        out_shape=jax.ShapeDtypeStruct((b, h, tk), F32),
        grid=(b,),
        in_specs=[pl.BlockSpec((1, h, tk), lambda i: (i, 0, 0))],
        out_specs=pl.BlockSpec((1, h, tk), lambda i: (i, 0, 0)),
        compiler_params=_params(("parallel",), 32),
        name="logf_cumsum",
    )(lf_rows)


def _rope(x, cos, sin):
    half = x.shape[-1] // 2
    x1, x2 = x[:, :half], x[:, half:]
    return jnp.concatenate([x1 * cos - x2 * sin, x1 * sin + x2 * cos], axis=-1)


def _retention_kernel(q_ref, k_ref, v_ref, g_ref, cos_ref, sin_ref, intra_ref, qdec_ref, kdec_ref,
                      sdec_ref, s0_ref, o_ref, s_out_ref, state_ref, *, chunks, heads, scale):
    i = pl.program_id(2)

    @pl.when(i == 0)
    def _():
        state_ref[...] = s0_ref[0]

    for c in range(chunks):
        rows = slice(c * CHUNK, (c + 1) * CHUNK)
        cos = cos_ref[rows, :]
        sin = sin_ref[rows, :]
        for h in range(heads):
            cols = slice(h * DK_RET, (h + 1) * DK_RET)
            q = _rope(q_ref[rows, cols], cos, sin)
            k = _rope(k_ref[rows, cols], cos, sin) * scale
            qb = q.astype(BF16)
            vb = v_ref[rows, cols].astype(BF16)
            state = state_ref[h]
            sc = lax.dot_general(qb, k.astype(BF16), _NT, preferred_element_type=F32) * intra_ref[h]
            o = (jnp.dot(sc.astype(BF16), vb, preferred_element_type=F32)
                 + jnp.dot(qb, state.astype(BF16), preferred_element_type=F32) * qdec_ref[h])
            kd = (k * kdec_ref[h]).astype(BF16)
            state_ref[h] = state * sdec_ref[h] + lax.dot_general(kd, vb, _TN, preferred_element_type=F32)
            o = o * lax.rsqrt(jnp.mean(o * o, axis=-1, keepdims=True) + EPS)
            gate = g_ref[rows, cols]
            o_ref[rows, cols] = (o * (gate * _sigmoid(gate))).astype(o_ref.dtype)

    @pl.when(i == pl.num_programs(2) - 1)
    def _():
        s_out_ref[0] = state_ref[...]


def _retention(proj, cos, sin, decay, state, dest, state_dest, *, layer, out_layer, n_layers, batch, seq,
               row_off, chunks, heads):
    tblk = chunks * CHUNK
    nblk = seq // tblk
    off = row_off // tblk
    intra, q_dec, k_dec, s_dec = decay

    def col(base):
        return pl.BlockSpec((tblk, heads * DK_RET),
                            lambda b, h, i: (off + b * nblk + i, base // heads + h))

    per_head = lambda shape: pl.BlockSpec((heads,) + shape, lambda b, h, i: (h, 0, 0))
    table = pl.BlockSpec((tblk, DK_RET // 2), lambda b, h, i: (i, 0))
    state_block = (None, 1, heads, DK_RET, DK_RET)
    call = _call_into(
        functools.partial(_retention_kernel, chunks=chunks, heads=heads, scale=DK_RET ** -0.5),
        {0: dest, 1: state_dest}, 11,
        out_shape=(jax.ShapeDtypeStruct((proj.shape[0], H_RET * DK_RET), BF16),
                   jax.ShapeDtypeStruct((n_layers, batch, H_RET, DK_RET, DK_RET), F32)),
        grid=(batch, H_RET // heads, nblk),
        in_specs=[col(0), col(H_RET), col(2 * H_RET), col(3 * H_RET), table, table,
                  per_head((CHUNK, CHUNK)), per_head((CHUNK, 1)), per_head((CHUNK, 1)),
                  per_head((1, 1)),
                  pl.BlockSpec(state_block, lambda b, h, i: (layer, b, h, 0, 0))],
        out_specs=(col(0), pl.BlockSpec(state_block, lambda b, h, i: (out_layer, b, h, 0, 0))),
        scratch_shapes=[pltpu.VMEM((heads, DK_RET, DK_RET), F32)],
        compiler_params=_params(("parallel", "parallel", "arbitrary"), 32),
        name="retention",
    )
    return call(proj, proj, proj, proj, cos, sin, intra, q_dec, k_dec, s_dec, state)


def _fox_prompt_kernel(qi_ref, kj_ref, q_ref, k_ref, v_ref, cq_ref, ck_ref, o_ref, m_ref, l_ref, acc_ref, *,
                       tq, scale):
    qi = qi_ref[pl.program_id(1)]
    kj = kj_ref[pl.program_id(1)]

    @pl.when(kj == 0)
    def _():
        m_ref[...] = jnp.full(m_ref.shape, NEG_INF, F32)
        l_ref[...] = jnp.zeros(l_ref.shape, F32)
        acc_ref[...] = jnp.zeros(acc_ref.shape, F32)

    def step(masked):
        if masked:
            key = lax.broadcasted_iota(jnp.int32, (tq, tq), 0)
            qry = lax.broadcasted_iota(jnp.int32, (tq, tq), 1)
            visible = key <= qry
        for h in range(H_FOX):
            cols = slice(h * DH_FOX, (h + 1) * DH_FOX)
            q = (q_ref[:, cols] * (scale * LOG2E)).astype(BF16)
            k = k_ref[:, cols].astype(BF16)
            bias = cq_ref[0, h:h + 1, :] * LOG2E - ck_ref[0, :, h:h + 1] * LOG2E
            s = lax.dot_general(k, q, _NT, preferred_element_type=F32) + bias
            if masked:
                s = jnp.where(visible, s, NEG_INF)
            m_prev = m_ref[h]
            m_new = jnp.maximum(m_prev, jnp.max(s, axis=0, keepdims=True))
            p = jnp.exp2(s - m_new)
            alpha = jnp.exp2(m_prev - m_new)
            l_ref[h] = alpha * l_ref[h] + jnp.sum(p, axis=0, keepdims=True)
            pv = lax.dot_general(v_ref[:, cols].astype(BF16), p.astype(BF16), _TN,
                                 preferred_element_type=F32)
            acc_ref[h] = alpha * acc_ref[h] + pv
            m_ref[h] = m_new

    @pl.when(kj < qi)
    def _():
        step(False)

    @pl.when(kj == qi)
    def _():
        step(True)
        for h in range(H_FOX):
            o = (acc_ref[h] / l_ref[h]).T
            o_ref[:, h * DH_FOX:(h + 1) * DH_FOX] = o.astype(o_ref.dtype)


def _fox_prompt(proj, c_col, c_row, *, batch, seq, row_off, tq, col_base):
    nq = seq // tq
    off = row_off // tq
    width = H_FOX * DH_FOX
    qb, kb, vb = (col_base + n for n in range(3))

    pairs = [(i, j) for i in range(nq) for j in range(i + 1)]
    qi_tab = jnp.asarray([i for i, _ in pairs], jnp.int32)
    kj_tab = jnp.asarray([j for _, j in pairs], jnp.int32)
    grid_spec = pltpu.PrefetchScalarGridSpec(
        num_scalar_prefetch=2,
        grid=(batch, len(pairs)),
        in_specs=[
            pl.BlockSpec((tq, width), lambda b, s, qi, kj: (off + b * nq + qi[s], qb)),
            pl.BlockSpec((tq, width), lambda b, s, qi, kj: (off + b * nq + kj[s], kb)),
            pl.BlockSpec((tq, width), lambda b, s, qi, kj: (off + b * nq + kj[s], vb)),
            pl.BlockSpec((1, H_FOX, tq), lambda b, s, qi, kj: (b, 0, qi[s])),
            pl.BlockSpec((1, tq, H_FOX), lambda b, s, qi, kj: (b, kj[s], 0)),
        ],
        out_specs=pl.BlockSpec((tq, width), lambda b, s, qi, kj: (off + b * nq + qi[s], 0)),
        scratch_shapes=[pltpu.VMEM((H_FOX, 1, tq), F32), pltpu.VMEM((H_FOX, 1, tq), F32),
                        pltpu.VMEM((H_FOX, DH_FOX, tq), F32)],
    )
    return pl.pallas_call(
        functools.partial(_fox_prompt_kernel, tq=tq, scale=DH_FOX ** -0.5),
        out_shape=jax.ShapeDtypeStruct((proj.shape[0], width), BF16),
        grid_spec=grid_spec,
        compiler_params=_params(("parallel", "arbitrary"), 48),
        name="fox_prompt",
    )(qi_tab, kj_tab, proj, proj, proj, c_row, c_col)


def _fox_sample_kernel(q_ref, kc_ref, vc_ref, kn_ref, vn_ref, cq_ref, ckc_ref, ckn_ref, o_ref,
                       m_ref, l_ref, acc_ref, s_ref, p_ref, *, tq, tkc, scale):
    j = pl.program_id(1)
    last = pl.num_programs(1) - 1

    @pl.when(j == 0)
    def _():
        m_ref[...] = jnp.full(m_ref.shape, NEG_INF, F32)
        l_ref[...] = jnp.zeros(l_ref.shape, F32)
        acc_ref[...] = jnp.zeros(acc_ref.shape, F32)

    cq = cq_ref[0] * LOG2E

    def scores(h, k, ck, visible=None):
        q = (q_ref[:, h * DH_FOX:(h + 1) * DH_FOX] * (scale * LOG2E)).astype(BF16)
        s = (lax.dot_general(q, k, _NT, preferred_element_type=F32)
             + (cq[:, h:h + 1] - ck[h:h + 1, :]))
        if visible is not None:
            s = jnp.where(visible, s, NEG_INF)
        s_ref[h * tq:(h + 1) * tq, :s.shape[1]] = s

    def softmax_update(width):
        s = s_ref[:, :width]
        m_prev = m_ref[...]
        m_new = jnp.maximum(m_prev, jnp.max(s, axis=-1, keepdims=True))
        p = jnp.exp2(s - m_new)
        alpha = jnp.exp2(m_prev - m_new)
        l_ref[...] = alpha * l_ref[...] + jnp.sum(p, axis=-1, keepdims=True)
        m_ref[...] = m_new
        p_ref[:, :width] = p.astype(BF16)
        return alpha

    def values(h, v, alpha, width):
        rows = slice(h * tq, (h + 1) * tq)
        acc_ref[h] = alpha[rows] * acc_ref[h] + jnp.dot(p_ref[rows, :width], v, preferred_element_type=F32)

    ck = ckc_ref[0] * LOG2E
    for h in range(H_FOX):
        scores(h, kc_ref[pl.ds(h, tkc, stride=H_FOX), :].astype(BF16), ck)
    alpha = softmax_update(tkc)
    for h in range(H_FOX):
        values(h, vc_ref[pl.ds(h, tkc, stride=H_FOX), :].astype(BF16), alpha, tkc)

    @pl.when(j == last)
    def _():
        row = lax.broadcasted_iota(jnp.int32, (tq, tq), 0)
        col = lax.broadcasted_iota(jnp.int32, (tq, tq), 1)
        visible = col <= row
        ck = ckn_ref[0] * LOG2E
        for h in range(H_FOX):
            scores(h, kn_ref[:, h * DH_FOX:(h + 1) * DH_FOX].astype(BF16), ck, visible)
        alpha = softmax_update(tq)
        for h in range(H_FOX):
            cols = slice(h * DH_FOX, (h + 1) * DH_FOX)
            values(h, vn_ref[:, cols].astype(BF16), alpha, tq)
            o_ref[:, cols] = (acc_ref[h] / l_ref[h * tq:(h + 1) * tq]).astype(o_ref.dtype)


def _fox_sample(proj, cache_k, cache_v, c_col, c_row_cache, c_row_new, dest, *, layer, batch, seq,
                row_off, tkc, col_base):
    past = cache_k.shape[2] // H_FOX
    nkc = past // tkc
    off = row_off // seq
    width = H_FOX * DH_FOX
    qb, kb, vb = (col_base + n for n in range(3))
    cache_spec = pl.BlockSpec((None, None, tkc * H_FOX, DH_FOX), lambda b, j: (layer, b, j, 0))

    call = _call_into(
        functools.partial(_fox_sample_kernel, tq=seq, tkc=tkc, scale=DH_FOX ** -0.5), {0: dest}, 8,
        out_shape=jax.ShapeDtypeStruct((proj.shape[0], width), BF16),
        grid=(batch, nkc),
        in_specs=[
            pl.BlockSpec((seq, width), lambda b, j: (off + b, qb)),
            cache_spec,
            cache_spec,
            pl.BlockSpec((seq, width), lambda b, j: (off + b, kb)),
            pl.BlockSpec((seq, width), lambda b, j: (off + b, vb)),
            pl.BlockSpec((1, seq, H_FOX), lambda b, j: (b, 0, 0)),
            pl.BlockSpec((1, H_FOX, tkc), lambda b, j: (b, 0, j)),
            pl.BlockSpec((1, H_FOX, seq), lambda b, j: (b, 0, 0)),
        ],
        out_specs=pl.BlockSpec((seq, width), lambda b, j: (off + b, 0)),
        scratch_shapes=[pltpu.VMEM((H_FOX * seq, 1), F32), pltpu.VMEM((H_FOX * seq, 1), F32),
                        pltpu.VMEM((H_FOX, seq, DH_FOX), F32),
                        pltpu.VMEM((H_FOX * seq, tkc), F32), pltpu.VMEM((H_FOX * seq, tkc), BF16)],
        compiler_params=_params(("parallel", "arbitrary"), 48),
        name="fox_sample",
    )
    return call(proj, cache_k, cache_v, proj, proj, c_col, c_row_cache, c_row_new)


def _out_proj_kernel(a_ref, b_ref, w_ref, x_ref, o_ref):
    ka = a_ref.shape[1]
    o_ref[...] = (x_ref[...]
                  + jnp.dot(a_ref[...], w_ref[:ka, :], preferred_element_type=F32)
                  + jnp.dot(b_ref[...], w_ref[ka:, :], preferred_element_type=F32))


def _out_proj(a, b, w, x, *, layer, tm):
    t, d = x.shape
    ka, kb = a.shape[1], b.shape[1]
    return pl.pallas_call(
        _out_proj_kernel,
        out_shape=jax.ShapeDtypeStruct((t, d), F32),
        grid=(t // tm,),
        in_specs=[
            pl.BlockSpec((tm, ka), lambda i: (i, 0)),
            pl.BlockSpec((tm, kb), lambda i: (i, 0)),
            pl.BlockSpec((None, ka + kb, d), lambda i: (layer, 0, 0)),
            pl.BlockSpec((tm, d), lambda i: (i, 0)),
        ],
        out_specs=pl.BlockSpec((tm, d), lambda i: (i, 0)),
        compiler_params=_params(("parallel",), 48),
        name="out_proj",
    )(a, b, w, x)


def _pool_kernel(x_ref, halo_ref, buf_ref, g_ref, w_ref, ps_ref, o_ref, bo_ref, full_ref, *, tm, pos0):
    i = pl.program_id(1)
    g = g_ref[...]
    x = x_ref[...]
    u = _rms(x, g)
    hist = jnp.where(i == 0, buf_ref[0], _rms(halo_ref[...], g))
    full_ref[0:POOL_HALO, :] = hist
    full_ref[POOL_HALO:POOL_HALO + tm, :] = u
    pos = pos0 + i * tm + lax.broadcasted_iota(jnp.int32, (tm, 1), 0)
    gc = x.shape[1] // len(POOL_WINDOWS)
    for n, win in enumerate(POOL_WINDOWS):
        cols = slice(n * gc, (n + 1) * gc)
        total = u[:, cols]
        for back in range(1, win):
            total = total + full_ref[POOL_HALO - back:POOL_HALO - back + tm, cols]
        inv_cnt = 1.0 / jnp.minimum(win, pos + 1).astype(F32)
        diff = (total * inv_cnt - u[:, cols]).astype(BF16)
        y = jnp.dot(diff, w_ref[n], preferred_element_type=F32) * ps_ref[:, cols]
        o_ref[:, cols] = x[:, cols] + y
    bo_ref[0] = u[tm - POOL_HALO:, :]


def _pool_mixer(x, buf, g, w, ps, dest, *, layer, mixer, batch, seq, row_off, tm, pos0):
    d = x.shape[1]
    nt = seq // tm
    off = row_off // tm
    per_halo = tm // POOL_HALO
    halo_off = row_off // POOL_HALO

    def halo_map(b, i):
        return (jnp.maximum(halo_off + (b * nt + i) * per_halo - 1, 0), 0)

    rows = pl.BlockSpec((tm, d), lambda b, i: (off + b * nt + i, 0))
    call = _call_into(
        functools.partial(_pool_kernel, tm=tm, pos0=pos0), {0: dest}, 6,
        out_shape=(jax.ShapeDtypeStruct(x.shape, F32),
                   jax.ShapeDtypeStruct((batch, POOL_HALO, d), F32)),
        grid=(batch, nt),
        in_specs=[
            rows,
            pl.BlockSpec((POOL_HALO, d), halo_map),
            pl.BlockSpec((1, POOL_HALO, d), lambda b, i: (b, 0, 0)),
            _layer_vec(layer, d),
            pl.BlockSpec((None,) + w.shape[1:], lambda b, i: (mixer, 0, 0, 0)),
            _layer_vec(mixer, d),
        ],
        out_specs=(rows, pl.BlockSpec((1, POOL_HALO, d), lambda b, i: (b, 0, 0))),
        scratch_shapes=[pltpu.VMEM((POOL_HALO + tm, d), F32)],
        compiler_params=_params(("parallel", "arbitrary"), 48),
        name="pool_mixer",
    )
    return call(x, x, buf, g, w, ps)


def _ffn_kernel(x_ref, g_ref, wg_ref, wu_ref, wd_ref, o_ref, h_ref):
    @pl.when(pl.program_id(1) == 0)
    def _():
        x = x_ref[...]
        h_ref[...] = _rms(x, g_ref[...]).astype(h_ref.dtype)
        o_ref[...] = x

    h = h_ref[...]
    a = jnp.dot(h, wg_ref[...].astype(BF16), preferred_element_type=F32)
    u = jnp.dot(h, wu_ref[...].astype(BF16), preferred_element_type=F32)
    act = (a * _sigmoid(a) * u).astype(BF16)
    o_ref[...] += jnp.dot(act, wd_ref[...].astype(BF16), preferred_element_type=F32)


def _ffn(x, g, wg, wu, wd, *, layer, tm, tf):
    t, d = x.shape
    f = wg.shape[2]
    return pl.pallas_call(
        _ffn_kernel,
        out_shape=jax.ShapeDtypeStruct((t, d), F32),
        grid=(t // tm, f // tf),
        in_specs=[
            pl.BlockSpec((tm, d), lambda i, j: (i, 0)),
            _layer_vec(layer, d),
            pl.BlockSpec((None, d, tf), lambda i, j: (layer, 0, j)),
            pl.BlockSpec((None, d, tf), lambda i, j: (layer, 0, j)),
            pl.BlockSpec((None, tf, d), lambda i, j: (layer, j, 0)),
        ],
        out_specs=pl.BlockSpec((tm, d), lambda i, j: (i, 0)),
        scratch_shapes=[pltpu.VMEM((tm, d), BF16)],
        compiler_params=_params(("parallel", "arbitrary"), 56),
        name="swiglu_ffn",
    )(x, g, wg, wu, wd)


def _ple_update(x_ref, g_ref, wg_ref, pp_ref, ps_ref, wp_ref, ntp):
    x = x_ref[...]
    h = _rms(x, g_ref[...]).astype(BF16)
    gate = _sigmoid(jnp.dot(h, wg_ref[...], preferred_element_type=F32))
    p = jnp.where(pl.program_id(0) < ntp, pp_ref[...], ps_ref[...])
    emb = jnp.dot(p.astype(BF16), wp_ref[...], preferred_element_type=F32)
    return x + gate * emb


def _ple_kernel(x_ref, g_ref, wg_ref, pp_ref, ps_ref, wp_ref, o_ref, *, ntp):
    o_ref[...] = _ple_update(x_ref, g_ref, wg_ref, pp_ref, ps_ref, wp_ref, ntp)


def _ple_final_kernel(x_ref, g_ref, wg_ref, pp_ref, ps_ref, wp_ref, gf_ref, yp_ref, ys_ref, *, ntp):
    y = _rms(_ple_update(x_ref, g_ref, wg_ref, pp_ref, ps_ref, wp_ref, ntp), gf_ref[...])

    @pl.when(pl.program_id(0) < ntp)
    def _():
        yp_ref[...] = y

    @pl.when(pl.program_id(0) >= ntp)
    def _():
        ys_ref[...] = y


def _ple(x, g, wg, p_prompt, p_sample, wp, final_g, *, layer, n_p, tm):
    t, d = x.shape
    pd = p_prompt.shape[2]
    ntp, nts = n_p // tm, (t - n_p) // tm
    rows = pl.BlockSpec((tm, d), lambda i: (i, 0))
    in_specs = [
        rows,
        _layer_vec(layer, d),
        pl.BlockSpec((None, d, d), lambda i: (layer, 0, 0)),
        pl.BlockSpec((None, tm, pd), lambda i: (layer, jnp.minimum(i, ntp - 1), 0)),
        pl.BlockSpec((None, tm, pd), lambda i: (layer, jnp.clip(i - ntp, 0, nts - 1), 0)),
        pl.BlockSpec((None, pd, d), lambda i: (layer, 0, 0)),
    ]
    if final_g is None:
        return pl.pallas_call(
            functools.partial(_ple_kernel, ntp=ntp),
            out_shape=jax.ShapeDtypeStruct((t, d), F32),
            grid=(t // tm,),
            in_specs=in_specs,
            out_specs=rows,
            compiler_params=_params(("parallel",), 56),
            name="gated_embedding",
        )(x, g, wg, p_prompt, p_sample, wp)
    return pl.pallas_call(
        functools.partial(_ple_final_kernel, ntp=ntp),
        out_shape=(jax.ShapeDtypeStruct((n_p, d), F32), jax.ShapeDtypeStruct((t - n_p, d), F32)),
        grid=(t // tm,),
        in_specs=in_specs + [pl.BlockSpec((1, d), lambda i: (0, 0))],
        out_specs=(pl.BlockSpec((tm, d), lambda i: (jnp.minimum(i, ntp - 1), 0)),
                   pl.BlockSpec((tm, d), lambda i: (jnp.clip(i - ntp, 0, nts - 1), 0))),
        compiler_params=_params(("arbitrary",), 56),
        name="gated_embedding_final_norm",
    )(x, g, wg, p_prompt, p_sample, wp, final_g)


def _rope_tables(pos0, seq):
    half = DK_RET // 2
    inv = ROPE_BASE ** (-jnp.arange(half, dtype=F32) / half)
    ang = (pos0 + jnp.arange(seq, dtype=jnp.int32)).astype(F32)[:, None] * inv[None, :]
    return jnp.cos(ang), jnp.sin(ang)


def _decay_tables():
    log_g = jnp.log1p(-jnp.power(2.0, -5.0 - jnp.arange(H_RET, dtype=F32)))
    idx = jnp.arange(CHUNK, dtype=F32)
    intra = jnp.exp(log_g[:, None, None] * jnp.abs(idx[:, None] - idx[None, :]))
    q_dec = jnp.exp(log_g[:, None] * (idx[None, :] + 1.0))[..., None]
    k_dec = jnp.exp(log_g[:, None] * (CHUNK - 1.0 - idx[None, :]))[..., None]
    s_dec = jnp.exp(log_g * CHUNK)[:, None, None]
    return intra, q_dec, k_dec, s_dec


def _tile(n, want):
    t = min(n, want)
    while n % t:
        t //= 2
    return t


def kernel(x_prompt, x_sample, p_prompt, p_sample, state_ret, cache_fox_k, cache_fox_v, cache_fox_logf,
           state_pool, norm_mix, w_in, b_forget, w_out, w_pool, pool_scale, norm_ffn, w_gate, w_up, w_down,
           norm_ple, ple_gate, ple_proj, final_norm):
    bp, tp, d = x_prompt.shape
    bs, ts, _ = x_sample.shape
    depth = norm_mix.shape[0]
    past = cache_fox_k.shape[2]
    n_p, n_s = bp * tp, bs * ts
    n_tok = n_p + n_s
    ret_w = H_RET * DK_RET
    fox_w = H_FOX * DH_FOX
    main_w = 4 * ret_w + 3 * fox_w
    fox_base = (4 * ret_w) // fox_w

    x = jnp.concatenate([x_prompt.reshape(n_p, d), x_sample.reshape(n_s, d)], axis=0)
    p_prompt = p_prompt.reshape(depth, n_p, -1)
    p_sample = p_sample.reshape(depth, n_s, -1)
    tm = _tile(n_tok, 512)

    as_rows = lambda a: a.reshape(a.shape[0], 1, a.shape[1])
    norm_mix, norm_ffn, norm_ple, pool_scale = map(as_rows, (norm_mix, norm_ffn, norm_ple, pool_scale))
    w_in_b, w_out_b, w_pool_b = w_in.astype(BF16), w_out.astype(BF16), w_pool.astype(BF16)
    ple_gate_b, ple_proj_b = ple_gate.astype(BF16), ple_proj.astype(BF16)
    w_f = jnp.pad(w_in[:, :, main_w:], ((0, 0), (0, 0), (0, LANES - H_FOX))).astype(BF16)
    b_f = jnp.pad(b_forget.astype(F32), ((0, 0), (0, LANES - H_FOX)))[:, None, :]
    cache_k = cache_fox_k.reshape(cache_fox_k.shape[:2] + (past * H_FOX, DH_FOX))
    cache_v = cache_fox_v.reshape(cache_fox_v.shape[:2] + (past * H_FOX, DH_FOX))

    decay = _decay_tables()
    rope_p = _rope_tables(0, tp)
    rope_s = _rope_tables(past, ts)
    zero_state = jnp.zeros((1, bp) + state_ret.shape[2:], F32)
    zero_buf = jnp.zeros((bp, POOL_HALO, d), F32)

    n_even = w_in.shape[0]
    kv_new = (None,) * 4
    ret_p = ret_s = None
    new_lf, new_pool_p, new_pool_s = [], [], []
    for i in range(depth):
        if i % 2 == 0:
            e = i // 2
            proj, lf, *kv_new = _in_proj(x, norm_mix, w_in_b, w_f, b_f, kv_new, layer=i, wlayer=e,
                                         n_layers=n_even, n=main_w, n_p=n_p, k_col=4 * ret_w + fox_w,
                                         tm=_tile(math.gcd(n_p, n_s), 1024))
            lf = lf[:, :H_FOX]
            lf_p = lf[:n_p].reshape(bp, tp, H_FOX)
            lf_s = lf[n_p:].reshape(bs, ts, H_FOX)
            c_row_p = _cumsum_rows(jnp.transpose(lf_p, (0, 2, 1)))
            lf_all_s = jnp.concatenate([cache_fox_logf[e].astype(F32), lf_s], axis=1)
            c_row_s = _cumsum_rows(jnp.transpose(lf_all_s, (0, 2, 1)))
            c_col_p = jnp.transpose(c_row_p, (0, 2, 1))
            c_col_s = jnp.transpose(c_row_s[:, :, past:], (0, 2, 1))

            o_r, ret_p = _retention(proj, *rope_p, decay, zero_state, None, ret_p, layer=0, out_layer=e,
                                    n_layers=n_even, batch=bp, seq=tp, row_off=0,
                                    chunks=_tile(tp, 512) // CHUNK, heads=2)
            o_r, ret_s = _retention(proj, *rope_s, decay, state_ret.astype(F32), o_r, ret_s, layer=e,
                                    out_layer=e, n_layers=n_even, batch=bs, seq=ts, row_off=n_p,
                                    chunks=ts // CHUNK, heads=H_RET)
            o_f = _fox_prompt(proj, c_col_p, c_row_p, batch=bp, seq=tp, row_off=0, tq=_tile(tp, 512),
                              col_base=fox_base)
            o_f = _fox_sample(proj, cache_k, cache_v, c_col_s, c_row_s, c_row_s[:, :, past:], o_f,
                              layer=e, batch=bs, seq=ts, row_off=n_p, tkc=_tile(past, 1024),
                              col_base=fox_base)
            x = _out_proj(o_r, o_f, w_out_b, x, layer=e, tm=tm)

            new_lf.append((lf_p, lf_s))
        else:
            o = i // 2
            buf_s = jnp.pad(state_pool[o].astype(F32), ((0, 0), (1, 0), (0, 0)))
            x_new, pool_p = _pool_mixer(x, zero_buf, norm_mix, w_pool_b, pool_scale, None, layer=i, mixer=o,
                                        batch=bp, seq=tp, row_off=0, tm=_tile(tp, 512), pos0=0)
            x, pool_s = _pool_mixer(x, buf_s, norm_mix, w_pool_b, pool_scale, x_new, layer=i, mixer=o,
                                    batch=bs, seq=ts, row_off=n_p, tm=ts, pos0=past)
            new_pool_p.append(pool_p[:, 1:])
            new_pool_s.append(pool_s[:, 1:])
        x = _ffn(x, norm_ffn, w_gate, w_up, w_down, layer=i, tm=_tile(n_tok, 1024), tf=256)
        x = _ple(x, norm_ple, ple_gate_b, p_prompt, p_sample, ple_proj_b,
                 final_norm[None, :] if i == depth - 1 else None, layer=i, n_p=n_p,
                 tm=_tile(math.gcd(n_p, n_s), 256))
    y_p, y_s = x
    k_p, k_s, v_p, v_s = kv_new
    prompt_kv = (n_even, bp, tp, H_FOX, DH_FOX)
    sample_kv = (n_even, bs, ts, H_FOX, DH_FOX)
    return (
        y_p.reshape(bp, tp, d),
        y_s.reshape(bs, ts, d),
        ret_p,
        ret_s.astype(state_ret.dtype),
        k_p.reshape(prompt_kv),
        k_s.reshape(sample_kv),
        v_p.reshape(prompt_kv),
        v_s.reshape(sample_kv),
        jnp.stack([lf[0] for lf in new_lf]),
        jnp.stack([lf[1] for lf in new_lf]),
        jnp.stack(new_pool_p),
        jnp.stack(new_pool_s),
    )
```

```python
import functools
import math

import jax
import jax.numpy as jnp
from jax import lax
from jax.experimental import pallas as pl
from jax.experimental.pallas import tpu as pltpu

F32 = jnp.float32
BF16 = jnp.bfloat16

EPS = 1e-6
NEG_INF = -1e30
ROPE_BASE = 10000.0
LOG2E = math.log2(math.e)
CHUNK = 64
H_RET = 4
DK_RET = 256
H_FOX = 8
DH_FOX = 128
POOL_WINDOWS = (2, 4, 8, 16)
POOL_HALO = 16
POOL_PAD = 8
assert POOL_WINDOWS == tuple(2 ** (n + 1) for n in range(len(POOL_WINDOWS))) and POOL_WINDOWS[-2] <= POOL_PAD
LANES = 128
CUMSUM_CHUNK = 256

_NT = (((1,), (1,)), ((), ()))
_TN = (((0,), (0,)), ((), ()))


def _params(semantics, vmem_mib):
    return pltpu.CompilerParams(dimension_semantics=semantics, vmem_limit_bytes=vmem_mib << 20)


def _rms(x, g):
    return x * lax.rsqrt(jnp.mean(x * x, axis=-1, keepdims=True) + EPS) * g


def _sigmoid(x):
    return 1.0 / (1.0 + jnp.exp(-x))


def _layer_vec(layer, d):
    return pl.BlockSpec((None, 1, d), lambda *_: (layer, 0, 0))


def _call_into(kernel, dests, n_in, **kwargs):
    dests = {k: a for k, a in dests.items() if a is not None}
    if not dests:
        return pl.pallas_call(kernel, **kwargs)
    n_d = len(dests)

    def body(*refs):
        kernel(*refs[:n_in], *refs[n_in + n_d:])

    kwargs["in_specs"] = list(kwargs["in_specs"]) + [pl.BlockSpec(memory_space=pl.ANY)] * n_d
    aliases = {n_in + pos: out for pos, out in enumerate(dests)}
    call = pl.pallas_call(body, input_output_aliases=aliases, **kwargs)
    return lambda *args: call(*args, *dests.values())


def _in_proj_kernel(x_ref, g_ref, w_ref, wf_ref, bf_ref, o_ref, lf_ref, kp_ref, ks_ref, vp_ref, vs_ref,
                    h_ref, *, tm, jk, ntp):
    i = pl.program_id(0)
    j = pl.program_id(1)

    @pl.when(j == 0)
    def _():
        h = _rms(x_ref[...], g_ref[...]).astype(h_ref.dtype)
        h_ref[...] = h
        z = jnp.dot(h, wf_ref[...], preferred_element_type=F32) + bf_ref[...]
        lf_ref[...] = jnp.minimum(z, 0.0) - jnp.log1p(jnp.exp(-jnp.abs(z)))

    o_ref[...] = jnp.dot(h_ref[...], w_ref[...], preferred_element_type=F32)

    def to_head_rows(dst_ref):
        for hd in range(H_FOX):
            dst_ref[pl.ds(hd, tm, stride=H_FOX), :] = o_ref[:, hd * DH_FOX:(hd + 1) * DH_FOX]

    for jj, prompt_ref, sample_ref in ((jk, kp_ref, ks_ref), (jk + 1, vp_ref, vs_ref)):
        @pl.when((j == jj) & (i < ntp))
        def _():
            to_head_rows(prompt_ref)

        @pl.when((j == jj) & (i >= ntp))
        def _():
            to_head_rows(sample_ref)


def _in_proj(x, g, w, wf, bf, kv_dests, *, layer, wlayer, n_layers, n, n_p, k_col, tm):
    t, d = x.shape
    n_s = t - n_p
    tn = H_FOX * DH_FOX
    ntp, nts = n_p // tm, n_s // tm
    rows = (tm * H_FOX, DH_FOX)
    prompt_rows = pl.BlockSpec(rows, lambda i, j: (wlayer * ntp + jnp.minimum(i, ntp - 1), 0))
    sample_rows = pl.BlockSpec(rows, lambda i, j: (wlayer * nts + jnp.clip(i - ntp, 0, nts - 1), 0))
    kv_p = jax.ShapeDtypeStruct((n_layers * n_p * H_FOX, DH_FOX), F32)
    kv_s = jax.ShapeDtypeStruct((n_layers * n_s * H_FOX, DH_FOX), F32)
    lanes = wf.shape[2]
    call = _call_into(
        functools.partial(_in_proj_kernel, tm=tm, jk=k_col // tn, ntp=ntp),
        {2 + k: a for k, a in enumerate(kv_dests)}, 5,
        out_shape=(jax.ShapeDtypeStruct((t, n), F32), jax.ShapeDtypeStruct((t, lanes), F32),
                   kv_p, kv_s, kv_p, kv_s),
        grid=(t // tm, n // tn),
        in_specs=[
            pl.BlockSpec((tm, d), lambda i, j: (i, 0)),
            _layer_vec(layer, d),
            pl.BlockSpec((None, d, tn), lambda i, j: (wlayer, 0, j)),
            pl.BlockSpec((None, d, lanes), lambda i, j: (wlayer, 0, 0)),
            pl.BlockSpec((None, 1, lanes), lambda i, j: (wlayer, 0, 0)),
        ],
        out_specs=(pl.BlockSpec((tm, tn), lambda i, j: (i, j)),
                   pl.BlockSpec((tm, lanes), lambda i, j: (i, 0)),
                   prompt_rows, sample_rows, prompt_rows, sample_rows),
        scratch_shapes=[pltpu.VMEM((tm, d), BF16)],
        compiler_params=_params(("arbitrary", "arbitrary"), 48),
        name="in_proj",
    )
    return call(x, g, w, wf, bf)


def _cumsum_kernel(lf_ref, c_ref, *, tk):
    r = lax.broadcasted_iota(jnp.int32, (CUMSUM_CHUNK, CUMSUM_CHUNK), 0)
    c = lax.broadcasted_iota(jnp.int32, (CUMSUM_CHUNK, CUMSUM_CHUNK), 1)
    upper = (r <= c).astype(F32)
    for b in range(lf_ref.shape[0]):
        carry = jnp.zeros((H_FOX, 1), F32)
        for s in range(0, tk, CUMSUM_CHUNK):
            w = min(CUMSUM_CHUNK, tk - s)
            blk = lf_ref[b, :, s:s + w]
            cs = jnp.dot(blk, upper[:w, :w], precision=lax.Precision.HIGHEST,
                         preferred_element_type=F32) + carry
            c_ref[b, :, s:s + w] = cs
            carry = cs[:, w - 1:w]


def _cumsum_rows(lf_rows):
    b, h, tk = lf_rows.shape
    bb = math.gcd(b, 8)
    return pl.pallas_call(
        functools.partial(_cumsum_kernel, tk=tk),
        out_shape=jax.ShapeDtypeStruct((b, h, tk), F32),
        grid=(b // bb,),
        in_specs=[pl.BlockSpec((bb, h, tk), lambda i: (i, 0, 0))],
        out_specs=pl.BlockSpec((bb, h, tk), lambda i: (i, 0, 0)),
        compiler_params=_params(("parallel",), 32),
        name="logf_cumsum",
    )(lf_rows)


def _rope(x, cos, sin):
    half = x.shape[-1] // 2
    x1, x2 = x[:, :half], x[:, half:]
    return jnp.concatenate([x1 * cos - x2 * sin, x1 * sin + x2 * cos], axis=-1)


def _retention_kernel(q_ref, k_ref, v_ref, g_ref, cos_ref, sin_ref, intra_ref, qdec_ref, kdec_ref,
                      sdec_ref, s0_ref, o_ref, s_out_ref, state_ref, *, chunks, heads, scale):
    i = pl.program_id(2)

    @pl.when(i == 0)
    def _():
        state_ref[...] = s0_ref[0]

    for c in range(chunks):
        rows = slice(c * CHUNK, (c + 1) * CHUNK)
        cos = cos_ref[rows, :]
        sin = sin_ref[rows, :]
        for h in range(heads):
            cols = slice(h * DK_RET, (h + 1) * DK_RET)
            q = _rope(q_ref[rows, cols], cos, sin)
            k = _rope(k_ref[rows, cols], cos, sin) * scale
            qb = q.astype(BF16)
            vb = v_ref[rows, cols].astype(BF16)
            state = state_ref[h]
            sc = lax.dot_general(qb, k.astype(BF16), _NT, preferred_element_type=F32) * intra_ref[h]
            o = (jnp.dot(sc.astype(BF16), vb, preferred_element_type=F32)
                 + jnp.dot(qb, state.astype(BF16), preferred_element_type=F32) * qdec_ref[h])
            kd = (k * kdec_ref[h]).astype(BF16)
            state_ref[h] = state * sdec_ref[h] + lax.dot_general(kd, vb, _TN, preferred_element_type=F32)
            o = o * lax.rsqrt(jnp.mean(o * o, axis=-1, keepdims=True) + EPS)
            gate = g_ref[rows, cols]
            o_ref[rows, cols] = (o * (gate * _sigmoid(gate))).astype(o_ref.dtype)

    @pl.when(i == pl.num_programs(2) - 1)
    def _():
        s_out_ref[0] = state_ref[...]


def _retention(proj, cos, sin, decay, state, dest, state_dest, *, layer, out_layer, n_layers, batch, seq,
               row_off, chunks, heads):
    tblk = chunks * CHUNK
    nblk = seq // tblk
    off = row_off // tblk
    intra, q_dec, k_dec, s_dec = decay

    def col(base):
        return pl.BlockSpec((tblk, heads * DK_RET),
                            lambda b, h, i: (off + b * nblk + i, base // heads + h))

    per_head = lambda shape: pl.BlockSpec((heads,) + shape, lambda b, h, i: (h, 0, 0))
    table = pl.BlockSpec((tblk, DK_RET // 2), lambda b, h, i: (i, 0))
    state_block = (None, 1, heads, DK_RET, DK_RET)
    call = _call_into(
        functools.partial(_retention_kernel, chunks=chunks, heads=heads, scale=DK_RET ** -0.5),
        {0: dest, 1: state_dest}, 11,
        out_shape=(jax.ShapeDtypeStruct((proj.shape[0], H_RET * DK_RET), BF16),
                   jax.ShapeDtypeStruct((n_layers, batch, H_RET, DK_RET, DK_RET), F32)),
        grid=(batch, H_RET // heads, nblk),
        in_specs=[col(0), col(H_RET), col(2 * H_RET), col(3 * H_RET), table, table,
                  per_head((CHUNK, CHUNK)), per_head((CHUNK, 1)), per_head((CHUNK, 1)),
                  per_head((1, 1)),
                  pl.BlockSpec(state_block, lambda b, h, i: (layer, b, h, 0, 0))],
        out_specs=(col(0), pl.BlockSpec(state_block, lambda b, h, i: (out_layer, b, h, 0, 0))),
        scratch_shapes=[pltpu.VMEM((heads, DK_RET, DK_RET), F32)],
        compiler_params=_params(("parallel", "parallel", "arbitrary"), 32),
        name="retention",
    )
    return call(proj, proj, proj, proj, cos, sin, intra, q_dec, k_dec, s_dec, state)


def _fox_prompt_kernel(qi_ref, kj_ref, q_ref, k_ref, v_ref, cq_ref, ck_ref, o_ref, m_ref, l_ref, acc_ref, *,
                       tq, scale):
    qi = qi_ref[pl.program_id(1)]
    kj = kj_ref[pl.program_id(1)]

    @pl.when(kj == 0)
    def _():
        m_ref[...] = jnp.full(m_ref.shape, NEG_INF, F32)
        l_ref[...] = jnp.zeros(l_ref.shape, F32)
        acc_ref[...] = jnp.zeros(acc_ref.shape, F32)

    def step(masked):
        if masked:
            key = lax.broadcasted_iota(jnp.int32, (tq, tq), 0)
            qry = lax.broadcasted_iota(jnp.int32, (tq, tq), 1)
            visible = key <= qry
        for h in range(H_FOX):
            cols = slice(h * DH_FOX, (h + 1) * DH_FOX)
            q = (q_ref[:, cols] * (scale * LOG2E)).astype(BF16)
            k = k_ref[:, cols].astype(BF16)
            cq = cq_ref[0, h:h + 1, :] * LOG2E
            s = lax.dot_general(k, q, _NT, preferred_element_type=F32) - ck_ref[0, :, h:h + 1] * LOG2E
            if masked:
                s = jnp.where(visible, s, NEG_INF)
            m_prev = m_ref[h]
            m_new = jnp.maximum(m_prev, jnp.max(s, axis=0, keepdims=True) + cq)
            p = jnp.exp2(s - (m_new - cq))
            alpha = jnp.exp2(m_prev - m_new)
            l_ref[h] = alpha * l_ref[h] + jnp.sum(p, axis=0, keepdims=True)
            pv = lax.dot_general(v_ref[:, cols].astype(BF16), p.astype(BF16), _TN,
                                 preferred_element_type=F32)
            acc_ref[h] = alpha * acc_ref[h] + pv
            m_ref[h] = m_new

    @pl.when(kj < qi)
    def _():
        step(False)

    @pl.when(kj == qi)
    def _():
        step(True)
        for h in range(H_FOX):
            o = (acc_ref[h] / l_ref[h]).T
            o_ref[:, h * DH_FOX:(h + 1) * DH_FOX] = o.astype(o_ref.dtype)


def _fox_prompt(proj, c_col, c_row, *, batch, seq, row_off, tq, col_base):
    nq = seq // tq
    off = row_off // tq
    width = H_FOX * DH_FOX
    qb, kb, vb = (col_base + n for n in range(3))

    pairs = [(i, j) for i in range(nq) for j in range(i + 1)]
    qi_tab = jnp.asarray([i for i, _ in pairs], jnp.int32)
    kj_tab = jnp.asarray([j for _, j in pairs], jnp.int32)
    grid_spec = pltpu.PrefetchScalarGridSpec(
        num_scalar_prefetch=2,
        grid=(batch, len(pairs)),
        in_specs=[
            pl.BlockSpec((tq, width), lambda b, s, qi, kj: (off + b * nq + qi[s], qb)),
            pl.BlockSpec((tq, width), lambda b, s, qi, kj: (off + b * nq + kj[s], kb)),
            pl.BlockSpec((tq, width), lambda b, s, qi, kj: (off + b * nq + kj[s], vb)),
            pl.BlockSpec((1, H_FOX, tq), lambda b, s, qi, kj: (b, 0, qi[s])),
            pl.BlockSpec((1, tq, H_FOX), lambda b, s, qi, kj: (b, kj[s], 0)),
        ],
        out_specs=pl.BlockSpec((tq, width), lambda b, s, qi, kj: (off + b * nq + qi[s], 0)),
        scratch_shapes=[pltpu.VMEM((H_FOX, 1, tq), F32), pltpu.VMEM((H_FOX, 1, tq), F32),
                        pltpu.VMEM((H_FOX, DH_FOX, tq), F32)],
    )
    return pl.pallas_call(
        functools.partial(_fox_prompt_kernel, tq=tq, scale=DH_FOX ** -0.5),
        out_shape=jax.ShapeDtypeStruct((proj.shape[0], width), BF16),
        grid_spec=grid_spec,
        compiler_params=_params(("parallel", "arbitrary"), 48),
        name="fox_prompt",
    )(qi_tab, kj_tab, proj, proj, proj, c_row, c_col)


def _fox_sample_kernel(q_ref, kc_ref, vc_ref, kn_ref, vn_ref, cq_ref, ckc_ref, ckn_ref, o_ref,
                       m_ref, l_ref, acc_ref, s_ref, p_ref, *, tq, tkc, scale):
    j = pl.program_id(1)
    last = pl.num_programs(1) - 1

    @pl.when(j == 0)
    def _():
        m_ref[...] = jnp.full(m_ref.shape, NEG_INF, F32)
        l_ref[...] = jnp.zeros(l_ref.shape, F32)
        acc_ref[...] = jnp.zeros(acc_ref.shape, F32)

    cq = jnp.concatenate([cq_ref[0, :, h:h + 1] for h in range(H_FOX)], axis=0) * LOG2E

    def scores(h, k, ck, visible=None):
        q = (q_ref[:, h * DH_FOX:(h + 1) * DH_FOX] * (scale * LOG2E)).astype(BF16)
        s = lax.dot_general(q, k, _NT, preferred_element_type=F32) - ck[h:h + 1, :]
        if visible is not None:
            s = jnp.where(visible, s, NEG_INF)
        s_ref[h * tq:(h + 1) * tq, :s.shape[1]] = s

    def softmax_update(width):
        s = s_ref[:, :width]
        m_prev = m_ref[...]
        m_new = jnp.maximum(m_prev, jnp.max(s, axis=-1, keepdims=True) + cq)
        p = jnp.exp2(s - (m_new - cq))
        alpha = jnp.exp2(m_prev - m_new)
        l_ref[...] = alpha * l_ref[...] + jnp.sum(p, axis=-1, keepdims=True)
        m_ref[...] = m_new
        p_ref[:, :width] = p.astype(BF16)
        return alpha

    def values(h, v, alpha, width):
        rows = slice(h * tq, (h + 1) * tq)
        acc_ref[h] = alpha[rows] * acc_ref[h] + jnp.dot(p_ref[rows, :width], v, preferred_element_type=F32)

    ck = ckc_ref[0] * LOG2E
    for h in range(H_FOX):
        scores(h, kc_ref[pl.ds(h, tkc, stride=H_FOX), :].astype(BF16), ck)
    alpha = softmax_update(tkc)
    for h in range(H_FOX):
        values(h, vc_ref[pl.ds(h, tkc, stride=H_FOX), :].astype(BF16), alpha, tkc)

    @pl.when(j == last)
    def _():
        row = lax.broadcasted_iota(jnp.int32, (tq, tq), 0)
        col = lax.broadcasted_iota(jnp.int32, (tq, tq), 1)
        visible = col <= row
        ck = ckn_ref[0] * LOG2E
        for h in range(H_FOX):
            scores(h, kn_ref[:, h * DH_FOX:(h + 1) * DH_FOX].astype(BF16), ck, visible)
        alpha = softmax_update(tq)
        for h in range(H_FOX):
            cols = slice(h * DH_FOX, (h + 1) * DH_FOX)
            values(h, vn_ref[:, cols].astype(BF16), alpha, tq)
            o_ref[:, cols] = (acc_ref[h] / l_ref[h * tq:(h + 1) * tq]).astype(o_ref.dtype)


def _fox_sample(proj, cache_k, cache_v, c_col, c_row_cache, c_row_new, dest, *, layer, batch, seq,
                row_off, tkc, col_base):
    past = cache_k.shape[2] // H_FOX
    nkc = past // tkc
    off = row_off // seq
    width = H_FOX * DH_FOX
    qb, kb, vb = (col_base + n for n in range(3))
    cache_spec = pl.BlockSpec((None, None, tkc * H_FOX, DH_FOX), lambda b, j: (layer, b, j, 0))

    call = _call_into(
        functools.partial(_fox_sample_kernel, tq=seq, tkc=tkc, scale=DH_FOX ** -0.5), {0: dest}, 8,
        out_shape=jax.ShapeDtypeStruct((proj.shape[0], width), BF16),
        grid=(batch, nkc),
        in_specs=[
            pl.BlockSpec((seq, width), lambda b, j: (off + b, qb)),
            cache_spec,
            cache_spec,
            pl.BlockSpec((seq, width), lambda b, j: (off + b, kb)),
            pl.BlockSpec((seq, width), lambda b, j: (off + b, vb)),
            pl.BlockSpec((1, seq, H_FOX), lambda b, j: (b, 0, 0)),
            pl.BlockSpec((1, H_FOX, tkc), lambda b, j: (b, 0, j)),
            pl.BlockSpec((1, H_FOX, seq), lambda b, j: (b, 0, 0)),
        ],
        out_specs=pl.BlockSpec((seq, width), lambda b, j: (off + b, 0)),
        scratch_shapes=[pltpu.VMEM((H_FOX * seq, 1), F32), pltpu.VMEM((H_FOX * seq, 1), F32),
                        pltpu.VMEM((H_FOX, seq, DH_FOX), F32),
                        pltpu.VMEM((H_FOX * seq, tkc), F32), pltpu.VMEM((H_FOX * seq, tkc), BF16)],
        compiler_params=_params(("parallel", "arbitrary"), 48),
        name="fox_sample",
    )
    return call(proj, cache_k, cache_v, proj, proj, c_col, c_row_cache, c_row_new)


def _out_proj_kernel(a_ref, b_ref, w_ref, x_ref, o_ref):
    ka = a_ref.shape[1]
    o_ref[...] = (x_ref[...]
                  + jnp.dot(a_ref[...], w_ref[:ka, :], preferred_element_type=F32)
                  + jnp.dot(b_ref[...], w_ref[ka:, :], preferred_element_type=F32))


def _out_proj(a, b, w, x, *, layer, tm):
    t, d = x.shape
    ka, kb = a.shape[1], b.shape[1]
    return pl.pallas_call(
        _out_proj_kernel,
        out_shape=jax.ShapeDtypeStruct((t, d), F32),
        grid=(t // tm,),
        in_specs=[
            pl.BlockSpec((tm, ka), lambda i: (i, 0)),
            pl.BlockSpec((tm, kb), lambda i: (i, 0)),
            pl.BlockSpec((None, ka + kb, d), lambda i: (layer, 0, 0)),
            pl.BlockSpec((tm, d), lambda i: (i, 0)),
        ],
        out_specs=pl.BlockSpec((tm, d), lambda i: (i, 0)),
        compiler_params=_params(("parallel",), 48),
        name="out_proj",
    )(a, b, w, x)


def _pool_kernel(x_ref, halo_ref, buf_ref, g_ref, w_ref, ps_ref, o_ref, bo_ref, full_ref, sum_a_ref,
                 sum_b_ref, *, tm, pos0):
    i = pl.program_id(1)
    g = g_ref[...]
    x = x_ref[...]
    u = _rms(x, g)
    hist = jnp.where(i == 0, buf_ref[0], _rms(halo_ref[...], g))
    lo, top = POOL_PAD, POOL_PAD + POOL_HALO
    end = top + tm
    for ref in (full_ref, sum_a_ref, sum_b_ref):
        ref[0:lo, :] = jnp.zeros((lo, x.shape[1]), F32)
    full_ref[lo:top, :] = hist
    full_ref[top:end, :] = u
    pos = pos0 + i * tm + lax.broadcasted_iota(jnp.int32, (tm, 1), 0)
    gc = x.shape[1] // len(POOL_WINDOWS)
    src, width = full_ref, 1
    for n, win in enumerate(POOL_WINDOWS):
        cols = slice(n * gc, (n + 1) * gc)
        if n == len(POOL_WINDOWS) - 1:
            total = src[top:end, cols] + src[top - width:end - width, cols]
        else:
            dst = (sum_a_ref, sum_b_ref)[n % 2]
            dst[lo:end, n * gc:] = src[lo:end, n * gc:] + src[lo - width:end - width, n * gc:]
            total = dst[top:end, cols]
            src = dst
        width = win
        inv_cnt = 1.0 / jnp.minimum(win, pos + 1).astype(F32)
        diff = (total * inv_cnt - u[:, cols]).astype(BF16)
        y = jnp.dot(diff, w_ref[n], preferred_element_type=F32) * ps_ref[:, cols]
        o_ref[:, cols] = x[:, cols] + y
    bo_ref[0] = u[tm - POOL_HALO:, :]


def _pool_mixer(x, buf, g, w, ps, dest, *, layer, mixer, batch, seq, row_off, tm, pos0):
    d = x.shape[1]
    nt = seq // tm
    off = row_off // tm
    per_halo = tm // POOL_HALO
    halo_off = row_off // POOL_HALO

    def halo_map(b, i):
        return (jnp.maximum(halo_off + (b * nt + i) * per_halo - 1, 0), 0)

    rows = pl.BlockSpec((tm, d), lambda b, i: (off + b * nt + i, 0))
    call = _call_into(
        functools.partial(_pool_kernel, tm=tm, pos0=pos0), {0: dest}, 6,
        out_shape=(jax.ShapeDtypeStruct(x.shape, F32),
                   jax.ShapeDtypeStruct((batch, POOL_HALO, d), F32)),
        grid=(batch, nt),
        in_specs=[
            rows,
            pl.BlockSpec((POOL_HALO, d), halo_map),
            pl.BlockSpec((1, POOL_HALO, d), lambda b, i: (b, 0, 0)),
            _layer_vec(layer, d),
            pl.BlockSpec((None,) + w.shape[1:], lambda b, i: (mixer, 0, 0, 0)),
            _layer_vec(mixer, d),
        ],
        out_specs=(rows, pl.BlockSpec((1, POOL_HALO, d), lambda b, i: (b, 0, 0))),
        scratch_shapes=[pltpu.VMEM((POOL_PAD + POOL_HALO + tm, d), F32)] * 3,
        compiler_params=_params(("parallel", "arbitrary"), 48),
        name="pool_mixer",
    )
    return call(x, x, buf, g, w, ps)


def _ffn_kernel(x_ref, g_ref, wg_ref, wu_ref, wd_ref, o_ref, h_ref):
    @pl.when(pl.program_id(1) == 0)
    def _():
        x = x_ref[...]
        h_ref[...] = _rms(x, g_ref[...]).astype(h_ref.dtype)
        o_ref[...] = x

    h = h_ref[...]
    a = jnp.dot(h, wg_ref[...].astype(BF16), preferred_element_type=F32)
    u = jnp.dot(h, wu_ref[...].astype(BF16), preferred_element_type=F32)
    act = (a * _sigmoid(a) * u).astype(BF16)
    o_ref[...] += jnp.dot(act, wd_ref[...].astype(BF16), preferred_element_type=F32)


def _ffn(x, g, wg, wu, wd, *, layer, tm, tf):
    t, d = x.shape
    f = wg.shape[2]
    return pl.pallas_call(
        _ffn_kernel,
        out_shape=jax.ShapeDtypeStruct((t, d), F32),
        grid=(t // tm, f // tf),
        in_specs=[
            pl.BlockSpec((tm, d), lambda i, j: (i, 0)),
            _layer_vec(layer, d),
            pl.BlockSpec((None, d, tf), lambda i, j: (layer, 0, j)),
            pl.BlockSpec((None, d, tf), lambda i, j: (layer, 0, j)),
            pl.BlockSpec((None, tf, d), lambda i, j: (layer, j, 0)),
        ],
        out_specs=pl.BlockSpec((tm, d), lambda i, j: (i, 0)),
        scratch_shapes=[pltpu.VMEM((tm, d), BF16)],
        compiler_params=_params(("parallel", "arbitrary"), 56),
        name="swiglu_ffn",
    )(x, g, wg, wu, wd)


def _ple_update(x_ref, g_ref, wg_ref, pp_ref, ps_ref, wp_ref, ntp):
    x = x_ref[...]
    h = _rms(x, g_ref[...]).astype(BF16)
    gate = _sigmoid(jnp.dot(h, wg_ref[...], preferred_element_type=F32))
    p = jnp.where(pl.program_id(0) < ntp, pp_ref[...], ps_ref[...])
    emb = jnp.dot(p.astype(BF16), wp_ref[...], preferred_element_type=F32)
    return x + gate * emb


def _ple_kernel(x_ref, g_ref, wg_ref, pp_ref, ps_ref, wp_ref, o_ref, *, ntp):
    o_ref[...] = _ple_update(x_ref, g_ref, wg_ref, pp_ref, ps_ref, wp_ref, ntp)


def _ple_final_kernel(x_ref, g_ref, wg_ref, pp_ref, ps_ref, wp_ref, gf_ref, yp_ref, ys_ref, *, ntp):
    y = _rms(_ple_update(x_ref, g_ref, wg_ref, pp_ref, ps_ref, wp_ref, ntp), gf_ref[...])

    @pl.when(pl.program_id(0) < ntp)
    def _():
        yp_ref[...] = y

    @pl.when(pl.program_id(0) >= ntp)
    def _():
        ys_ref[...] = y


def _ple(x, g, wg, p_prompt, p_sample, wp, final_g, *, layer, n_p, tm):
    t, d = x.shape
    pd = p_prompt.shape[2]
    ntp, nts = n_p // tm, (t - n_p) // tm
    rows = pl.BlockSpec((tm, d), lambda i: (i, 0))
    in_specs = [
        rows,
        _layer_vec(layer, d),
        pl.BlockSpec((None, d, d), lambda i: (layer, 0, 0)),
        pl.BlockSpec((None, tm, pd), lambda i: (layer, jnp.minimum(i, ntp - 1), 0)),
        pl.BlockSpec((None, tm, pd), lambda i: (layer, jnp.clip(i - ntp, 0, nts - 1), 0)),
        pl.BlockSpec((None, pd, d), lambda i: (layer, 0, 0)),
    ]
    if final_g is None:
        return pl.pallas_call(
            functools.partial(_ple_kernel, ntp=ntp),
            out_shape=jax.ShapeDtypeStruct((t, d), F32),
            grid=(t // tm,),
            in_specs=in_specs,
            out_specs=rows,
            compiler_params=_params(("parallel",), 56),
            name="gated_embedding",
        )(x, g, wg, p_prompt, p_sample, wp)
    return pl.pallas_call(
        functools.partial(_ple_final_kernel, ntp=ntp),
        out_shape=(jax.ShapeDtypeStruct((n_p, d), F32), jax.ShapeDtypeStruct((t - n_p, d), F32)),
        grid=(t // tm,),
        in_specs=in_specs + [pl.BlockSpec((1, d), lambda i: (0, 0))],
        out_specs=(pl.BlockSpec((tm, d), lambda i: (jnp.minimum(i, ntp - 1), 0)),
                   pl.BlockSpec((tm, d), lambda i: (jnp.clip(i - ntp, 0, nts - 1), 0))),
        compiler_params=_params(("arbitrary",), 56),
        name="gated_embedding_final_norm",
    )(x, g, wg, p_prompt, p_sample, wp, final_g)


def _rope_tables(pos0, seq):
    half = DK_RET // 2
    inv = ROPE_BASE ** (-jnp.arange(half, dtype=F32) / half)
    ang = (pos0 + jnp.arange(seq, dtype=jnp.int32)).astype(F32)[:, None] * inv[None, :]
    return jnp.cos(ang), jnp.sin(ang)


def _decay_tables():
    log_g = jnp.log1p(-jnp.power(2.0, -5.0 - jnp.arange(H_RET, dtype=F32)))
    idx = jnp.arange(CHUNK, dtype=F32)
    intra = jnp.exp(log_g[:, None, None] * jnp.abs(idx[:, None] - idx[None, :]))
    q_dec = jnp.exp(log_g[:, None] * (idx[None, :] + 1.0))[..., None]
    k_dec = jnp.exp(log_g[:, None] * (CHUNK - 1.0 - idx[None, :]))[..., None]
    s_dec = jnp.exp(log_g * CHUNK)[:, None, None]
    return intra, q_dec, k_dec, s_dec


def _tile(n, want):
    t = min(n, want)
    while n % t:
        t //= 2
    return t


def kernel(x_prompt, x_sample, p_prompt, p_sample, state_ret, cache_fox_k, cache_fox_v, cache_fox_logf,
           state_pool, norm_mix, w_in, b_forget, w_out, w_pool, pool_scale, norm_ffn, w_gate, w_up, w_down,
           norm_ple, ple_gate, ple_proj, final_norm):
    bp, tp, d = x_prompt.shape
    bs, ts, _ = x_sample.shape
    depth = norm_mix.shape[0]
    past = cache_fox_k.shape[2]
    n_p, n_s = bp * tp, bs * ts
    n_tok = n_p + n_s
    ret_w = H_RET * DK_RET
    fox_w = H_FOX * DH_FOX
    main_w = 4 * ret_w + 3 * fox_w
    fox_base = (4 * ret_w) // fox_w

    x = jnp.concatenate([x_prompt.reshape(n_p, d), x_sample.reshape(n_s, d)], axis=0)
    p_prompt = p_prompt.reshape(depth, n_p, -1)
    p_sample = p_sample.reshape(depth, n_s, -1)
    tm = _tile(n_tok, 512)

    as_rows = lambda a: a.reshape(a.shape[0], 1, a.shape[1])
    norm_mix, norm_ffn, norm_ple, pool_scale = map(as_rows, (norm_mix, norm_ffn, norm_ple, pool_scale))
    w_in_b, w_out_b, w_pool_b = w_in.astype(BF16), w_out.astype(BF16), w_pool.astype(BF16)
    ple_gate_b, ple_proj_b = ple_gate.astype(BF16), ple_proj.astype(BF16)
    w_f = jnp.pad(w_in[:, :, main_w:], ((0, 0), (0, 0), (0, LANES - H_FOX))).astype(BF16)
    b_f = jnp.pad(b_forget.astype(F32), ((0, 0), (0, LANES - H_FOX)))[:, None, :]
    cache_k = cache_fox_k.reshape(cache_fox_k.shape[:2] + (past * H_FOX, DH_FOX))
    cache_v = cache_fox_v.reshape(cache_fox_v.shape[:2] + (past * H_FOX, DH_FOX))

    decay = _decay_tables()
    rope_p = _rope_tables(0, tp)
    rope_s = _rope_tables(past, ts)
    zero_state = jnp.zeros((1, bp) + state_ret.shape[2:], F32)
    zero_buf = jnp.zeros((bp, POOL_HALO, d), F32)

    n_even = w_in.shape[0]
    kv_new = (None,) * 4
    ret_p = ret_s = None
    new_lf, new_pool_p, new_pool_s = [], [], []
    for i in range(depth):
        if i % 2 == 0:
            e = i // 2
            proj, lf, *kv_new = _in_proj(x, norm_mix, w_in_b, w_f, b_f, kv_new, layer=i, wlayer=e,
                                         n_layers=n_even, n=main_w, n_p=n_p, k_col=4 * ret_w + fox_w,
                                         tm=_tile(math.gcd(n_p, n_s), 512))
            lf = lf[:, :H_FOX]
            lf_p = lf[:n_p].reshape(bp, tp, H_FOX)
            lf_s = lf[n_p:].reshape(bs, ts, H_FOX)
            c_row_p = _cumsum_rows(jnp.transpose(lf_p, (0, 2, 1)))
            lf_all_s = jnp.concatenate([cache_fox_logf[e].astype(F32), lf_s], axis=1)
            c_row_s = _cumsum_rows(jnp.transpose(lf_all_s, (0, 2, 1)))
            c_col_p = jnp.transpose(c_row_p, (0, 2, 1))
            c_col_s = jnp.transpose(c_row_s[:, :, past:], (0, 2, 1))

            o_r, ret_p = _retention(proj, *rope_p, decay, zero_state, None, ret_p, layer=0, out_layer=e,
                                    n_layers=n_even, batch=bp, seq=tp, row_off=0,
                                    chunks=_tile(tp, 512) // CHUNK, heads=2)
            o_r, ret_s = _retention(proj, *rope_s, decay, state_ret.astype(F32), o_r, ret_s, layer=e,
                                    out_layer=e, n_layers=n_even, batch=bs, seq=ts, row_off=n_p,
                                    chunks=ts // CHUNK, heads=H_RET)
            o_f = _fox_prompt(proj, c_col_p, c_row_p, batch=bp, seq=tp, row_off=0, tq=_tile(tp, 512),
                              col_base=fox_base)
            o_f = _fox_sample(proj, cache_k, cache_v, c_col_s, c_row_s, c_row_s[:, :, past:], o_f,
                              layer=e, batch=bs, seq=ts, row_off=n_p, tkc=_tile(past, 1024),
                              col_base=fox_base)
            x = _out_proj(o_r, o_f, w_out_b, x, layer=e, tm=tm)

            new_lf.append((lf_p, lf_s))
        else:
            o = i // 2
            buf_s = jnp.pad(state_pool[o].astype(F32), ((0, 0), (1, 0), (0, 0)))
            x_new, pool_p = _pool_mixer(x, zero_buf, norm_mix, w_pool_b, pool_scale, None, layer=i, mixer=o,
                                        batch=bp, seq=tp, row_off=0, tm=_tile(tp, 512), pos0=0)
            x, pool_s = _pool_mixer(x, buf_s, norm_mix, w_pool_b, pool_scale, x_new, layer=i, mixer=o,
                                    batch=bs, seq=ts, row_off=n_p, tm=ts, pos0=past)
            new_pool_p.append(pool_p[:, 1:])
            new_pool_s.append(pool_s[:, 1:])
        x = _ffn(x, norm_ffn, w_gate, w_up, w_down, layer=i, tm=_tile(n_tok, 1024), tf=256)
        x = _ple(x, norm_ple, ple_gate_b, p_prompt, p_sample, ple_proj_b,
                 final_norm[None, :] if i == depth - 1 else None, layer=i, n_p=n_p,
                 tm=_tile(math.gcd(n_p, n_s), 256))
    y_p, y_s = x
    k_p, k_s, v_p, v_s = kv_new
    prompt_kv = (n_even, bp, tp, H_FOX, DH_FOX)
    sample_kv = (n_even, bs, ts, H_FOX, DH_FOX)
    return (
        y_p.reshape(bp, tp, d),
        y_s.reshape(bs, ts, d),
        ret_p,
        ret_s.astype(state_ret.dtype),
        k_p.reshape(prompt_kv),
        k_s.reshape(sample_kv),
        v_p.reshape(prompt_kv),
        v_s.reshape(sample_kv),
        jnp.stack([lf[0] for lf in new_lf]),
        jnp.stack([lf[1] for lf in new_lf]),
        jnp.stack(new_pool_p),
        jnp.stack(new_pool_s),
    )
```

```python
import functools
import math

import jax
import jax.numpy as jnp
from jax import lax
from jax.experimental import pallas as pl
from jax.experimental.pallas import tpu as pltpu

F32 = jnp.float32
BF16 = jnp.bfloat16

EPS = 1e-6
NEG_INF = -1e30
ROPE_BASE = 10000.0
LOG2E = math.log2(math.e)
CHUNK = 64
H_RET = 4
DK_RET = 256
H_FOX = 8
DH_FOX = 128
POOL_WINDOWS = (2, 4, 8, 16)
POOL_HALO = 16
POOL_PAD = 8
assert POOL_WINDOWS == tuple(2 ** (n + 1) for n in range(len(POOL_WINDOWS))) and POOL_WINDOWS[-2] <= POOL_PAD
LANES = 128
CUMSUM_CHUNK = 256

_NT = (((1,), (1,)), ((), ()))
_TN = (((0,), (0,)), ((), ()))


def _params(semantics, vmem_mib):
    return pltpu.CompilerParams(dimension_semantics=semantics, vmem_limit_bytes=vmem_mib << 20)


def _rms(x, g):
    return x * lax.rsqrt(jnp.mean(x * x, axis=-1, keepdims=True) + EPS) * g


def _sigmoid(x):
    return 1.0 / (1.0 + jnp.exp(-x))


def _layer_vec(layer, d):
    return pl.BlockSpec((None, 1, d), lambda *_: (layer, 0, 0))


def _call_into(kernel, dests, n_in, **kwargs):
    dests = {k: a for k, a in dests.items() if a is not None}
    if not dests:
        return pl.pallas_call(kernel, **kwargs)
    n_d = len(dests)

    def body(*refs):
        kernel(*refs[:n_in], *refs[n_in + n_d:])

    kwargs["in_specs"] = list(kwargs["in_specs"]) + [pl.BlockSpec(memory_space=pl.ANY)] * n_d
    aliases = {n_in + pos: out for pos, out in enumerate(dests)}
    call = pl.pallas_call(body, input_output_aliases=aliases, **kwargs)
    return lambda *args: call(*args, *dests.values())


def _in_proj_kernel(x_ref, g_ref, w_ref, wf_ref, bf_ref, o_ref, lf_ref, kp_ref, ks_ref, vp_ref, vs_ref,
                    h_ref, *, tm, jk, ntp):
    i = pl.program_id(0)
    j = pl.program_id(1)

    @pl.when(j == 0)
    def _():
        h = _rms(x_ref[...], g_ref[...]).astype(h_ref.dtype)
        h_ref[...] = h
        z = jnp.dot(h, wf_ref[...], preferred_element_type=F32) + bf_ref[...]
        lf_ref[...] = jnp.minimum(z, 0.0) - jnp.log1p(jnp.exp(-jnp.abs(z)))

    o_ref[...] = jnp.dot(h_ref[...], w_ref[...], preferred_element_type=F32)

    def to_head_rows(dst_ref):
        for hd in range(H_FOX):
            dst_ref[pl.ds(hd, tm, stride=H_FOX), :] = o_ref[:, hd * DH_FOX:(hd + 1) * DH_FOX]

    for jj, prompt_ref, sample_ref in ((jk, kp_ref, ks_ref), (jk + 1, vp_ref, vs_ref)):
        @pl.when((j == jj) & (i < ntp))
        def _():
            to_head_rows(prompt_ref)

        @pl.when((j == jj) & (i >= ntp))
        def _():
            to_head_rows(sample_ref)


def _in_proj(x, g, w, wf, bf, kv_dests, *, layer, wlayer, n_layers, n_p, k_tile, tm):
    t, d = x.shape
    n_s = t - n_p
    tiles, tn = w.shape[1], w.shape[3]
    ntp, nts = n_p // tm, n_s // tm
    rows = (tm * H_FOX, DH_FOX)
    prompt_rows = pl.BlockSpec(rows, lambda i, j: (wlayer * ntp + jnp.minimum(i, ntp - 1), 0))
    sample_rows = pl.BlockSpec(rows, lambda i, j: (wlayer * nts + jnp.clip(i - ntp, 0, nts - 1), 0))
    kv_p = jax.ShapeDtypeStruct((n_layers * n_p * H_FOX, DH_FOX), F32)
    kv_s = jax.ShapeDtypeStruct((n_layers * n_s * H_FOX, DH_FOX), F32)
    lanes = wf.shape[2]
    call = _call_into(
        functools.partial(_in_proj_kernel, tm=tm, jk=k_tile, ntp=ntp),
        {2 + k: a for k, a in enumerate(kv_dests)}, 5,
        out_shape=(jax.ShapeDtypeStruct((tiles, t, tn), F32), jax.ShapeDtypeStruct((t, lanes), F32),
                   kv_p, kv_s, kv_p, kv_s),
        grid=(t // tm, tiles),
        in_specs=[
            pl.BlockSpec((tm, d), lambda i, j: (i, 0)),
            _layer_vec(layer, d),
            pl.BlockSpec((None, None, d, tn), lambda i, j: (wlayer, j, 0, 0)),
            pl.BlockSpec((None, d, lanes), lambda i, j: (wlayer, 0, 0)),
            pl.BlockSpec((None, 1, lanes), lambda i, j: (wlayer, 0, 0)),
        ],
        out_specs=(pl.BlockSpec((None, tm, tn), lambda i, j: (j, i, 0)),
                   pl.BlockSpec((tm, lanes), lambda i, j: (i, 0)),
                   prompt_rows, sample_rows, prompt_rows, sample_rows),
        scratch_shapes=[pltpu.VMEM((tm, d), BF16)],
        compiler_params=_params(("arbitrary", "arbitrary"), 48),
        name="in_proj",
    )
    return call(x, g, w, wf, bf)


def _cumsum_kernel(lf_ref, c_ref, *, tk):
    r = lax.broadcasted_iota(jnp.int32, (CUMSUM_CHUNK, CUMSUM_CHUNK), 0)
    c = lax.broadcasted_iota(jnp.int32, (CUMSUM_CHUNK, CUMSUM_CHUNK), 1)
    upper = (r <= c).astype(F32)
    for b in range(lf_ref.shape[0]):
        carry = jnp.zeros((H_FOX, 1), F32)
        for s in range(0, tk, CUMSUM_CHUNK):
            w = min(CUMSUM_CHUNK, tk - s)
            blk = lf_ref[b, :, s:s + w]
            cs = jnp.dot(blk, upper[:w, :w], precision=lax.Precision.HIGHEST,
                         preferred_element_type=F32) + carry
            c_ref[b, :, s:s + w] = cs
            carry = cs[:, w - 1:w]


def _cumsum_rows(lf_rows):
    b, h, tk = lf_rows.shape
    bb = math.gcd(b, 8)
    return pl.pallas_call(
        functools.partial(_cumsum_kernel, tk=tk),
        out_shape=jax.ShapeDtypeStruct((b, h, tk), F32),
        grid=(b // bb,),
        in_specs=[pl.BlockSpec((bb, h, tk), lambda i: (i, 0, 0))],
        out_specs=pl.BlockSpec((bb, h, tk), lambda i: (i, 0, 0)),
        compiler_params=_params(("parallel",), 32),
        name="logf_cumsum",
    )(lf_rows)


def _rope(x, cos, sin):
    half = x.shape[-1] // 2
    x1, x2 = x[:, :half], x[:, half:]
    return jnp.concatenate([x1 * cos - x2 * sin, x1 * sin + x2 * cos], axis=-1)


def _retention_kernel(q_ref, k_ref, v_ref, g_ref, cos_ref, sin_ref, intra_ref, qdec_ref, kdec_ref,
                      sdec_ref, s0_ref, o_ref, s_out_ref, state_ref, *, chunks, heads, scale):
    i = pl.program_id(2)

    @pl.when(i == 0)
    def _():
        state_ref[...] = s0_ref[0]

    for c in range(chunks):
        rows = slice(c * CHUNK, (c + 1) * CHUNK)
        cos = cos_ref[rows, :]
        sin = sin_ref[rows, :]
        for h in range(heads):
            cols = slice(h * DK_RET, (h + 1) * DK_RET)
            q = _rope(q_ref[rows, cols], cos, sin)
            k = _rope(k_ref[rows, cols], cos, sin) * scale
            qb = q.astype(BF16)
            vb = v_ref[rows, cols].astype(BF16)
            state = state_ref[h]
            sc = lax.dot_general(qb, k.astype(BF16), _NT, preferred_element_type=F32) * intra_ref[h]
            o = (jnp.dot(sc.astype(BF16), vb, preferred_element_type=F32)
                 + jnp.dot(qb, state.astype(BF16), preferred_element_type=F32) * qdec_ref[h])
            kd = (k * kdec_ref[h]).astype(BF16)
            state_ref[h] = state * sdec_ref[h] + lax.dot_general(kd, vb, _TN, preferred_element_type=F32)
            o = o * lax.rsqrt(jnp.mean(o * o, axis=-1, keepdims=True) + EPS)
            gate = g_ref[rows, cols]
            o_ref[rows, cols] = (o * (gate * _sigmoid(gate))).astype(o_ref.dtype)

    @pl.when(i == pl.num_programs(2) - 1)
    def _():
        s_out_ref[0] = state_ref[...]


def _retention(proj, cos, sin, decay, state, dest, state_dest, *, layer, out_layer, n_layers, batch, seq,
               row_off, chunks, heads):
    tblk = chunks * CHUNK
    nblk = seq // tblk
    off = row_off // tblk
    intra, q_dec, k_dec, s_dec = decay
    head_cols = pl.BlockSpec((tblk, heads * DK_RET), lambda b, h, i: (off + b * nblk + i, h))

    def col(tile):
        return pl.BlockSpec((None, tblk, heads * DK_RET), lambda b, h, i: (tile, off + b * nblk + i, h))

    per_head = lambda shape: pl.BlockSpec((heads,) + shape, lambda b, h, i: (h, 0, 0))
    table = pl.BlockSpec((tblk, DK_RET // 2), lambda b, h, i: (i, 0))
    state_block = (None, 1, heads, DK_RET, DK_RET)
    call = _call_into(
        functools.partial(_retention_kernel, chunks=chunks, heads=heads, scale=DK_RET ** -0.5),
        {0: dest, 1: state_dest}, 11,
        out_shape=(jax.ShapeDtypeStruct((proj.shape[1], H_RET * DK_RET), BF16),
                   jax.ShapeDtypeStruct((n_layers, batch, H_RET, DK_RET, DK_RET), F32)),
        grid=(batch, H_RET // heads, nblk),
        in_specs=[col(0), col(1), col(2), col(3), table, table,
                  per_head((CHUNK, CHUNK)), per_head((CHUNK, 1)), per_head((CHUNK, 1)),
                  per_head((1, 1)),
                  pl.BlockSpec(state_block, lambda b, h, i: (layer, b, h, 0, 0))],
        out_specs=(head_cols, pl.BlockSpec(state_block, lambda b, h, i: (out_layer, b, h, 0, 0))),
        scratch_shapes=[pltpu.VMEM((heads, DK_RET, DK_RET), F32)],
        compiler_params=_params(("parallel", "parallel", "arbitrary"), 32),
        name="retention",
    )
    return call(proj, proj, proj, proj, cos, sin, intra, q_dec, k_dec, s_dec, state)


def _fox_prompt_kernel(qi_ref, kj_ref, q_ref, k_ref, v_ref, cq_ref, ck_ref, o_ref, m_ref, l_ref, acc_ref, *,
                       tq, scale):
    qi = qi_ref[pl.program_id(1)]
    kj = kj_ref[pl.program_id(1)]

    @pl.when(kj == 0)
    def _():
        m_ref[...] = jnp.full(m_ref.shape, NEG_INF, F32)
        l_ref[...] = jnp.zeros(l_ref.shape, F32)
        acc_ref[...] = jnp.zeros(acc_ref.shape, F32)

    def step(masked):
        if masked:
            key = lax.broadcasted_iota(jnp.int32, (tq, tq), 0)
            qry = lax.broadcasted_iota(jnp.int32, (tq, tq), 1)
            visible = key <= qry
        for h in range(H_FOX):
            cols = slice(h * DH_FOX, (h + 1) * DH_FOX)
            q = (q_ref[:, cols] * (scale * LOG2E)).astype(BF16)
            k = k_ref[:, cols].astype(BF16)
            cq = cq_ref[0, h:h + 1, :] * LOG2E
            s = lax.dot_general(k, q, _NT, preferred_element_type=F32) - ck_ref[0, :, h:h + 1] * LOG2E
            if masked:
                s = jnp.where(visible, s, NEG_INF)
            m_prev = m_ref[h]
            m_new = jnp.maximum(m_prev, jnp.max(s, axis=0, keepdims=True) + cq)
            p = jnp.exp2(s - (m_new - cq))
            alpha = jnp.exp2(m_prev - m_new)
            l_ref[h] = alpha * l_ref[h] + jnp.sum(p, axis=0, keepdims=True)
            pv = lax.dot_general(v_ref[:, cols].astype(BF16), p.astype(BF16), _TN,
                                 preferred_element_type=F32)
            acc_ref[h] = alpha * acc_ref[h] + pv
            m_ref[h] = m_new

    @pl.when(kj < qi)
    def _():
        step(False)

    @pl.when(kj == qi)
    def _():
        step(True)
        for h in range(H_FOX):
            o = (acc_ref[h] / l_ref[h]).T
            o_ref[:, h * DH_FOX:(h + 1) * DH_FOX] = o.astype(o_ref.dtype)


def _fox_prompt(proj, c_col, c_row, *, batch, seq, row_off, tq, col_base):
    nq = seq // tq
    off = row_off // tq
    width = H_FOX * DH_FOX
    qb, kb, vb = (col_base + n for n in range(3))

    pairs = [(i, j) for i in range(nq) for j in range(i + 1)]
    qi_tab = jnp.asarray([i for i, _ in pairs], jnp.int32)
    kj_tab = jnp.asarray([j for _, j in pairs], jnp.int32)
    grid_spec = pltpu.PrefetchScalarGridSpec(
        num_scalar_prefetch=2,
        grid=(batch, len(pairs)),
        in_specs=[
            pl.BlockSpec((None, tq, width), lambda b, s, qi, kj: (qb, off + b * nq + qi[s], 0)),
            pl.BlockSpec((None, tq, width), lambda b, s, qi, kj: (kb, off + b * nq + kj[s], 0)),
            pl.BlockSpec((None, tq, width), lambda b, s, qi, kj: (vb, off + b * nq + kj[s], 0)),
            pl.BlockSpec((1, H_FOX, tq), lambda b, s, qi, kj: (b, 0, qi[s])),
            pl.BlockSpec((1, tq, H_FOX), lambda b, s, qi, kj: (b, kj[s], 0)),
        ],
        out_specs=pl.BlockSpec((tq, width), lambda b, s, qi, kj: (off + b * nq + qi[s], 0)),
        scratch_shapes=[pltpu.VMEM((H_FOX, 1, tq), F32), pltpu.VMEM((H_FOX, 1, tq), F32),
                        pltpu.VMEM((H_FOX, DH_FOX, tq), F32)],
    )
    return pl.pallas_call(
        functools.partial(_fox_prompt_kernel, tq=tq, scale=DH_FOX ** -0.5),
        out_shape=jax.ShapeDtypeStruct((proj.shape[1], width), BF16),
        grid_spec=grid_spec,
        compiler_params=_params(("parallel", "arbitrary"), 48),
        name="fox_prompt",
    )(qi_tab, kj_tab, proj, proj, proj, c_row, c_col)


def _fox_sample_kernel(q_ref, kc_ref, vc_ref, kn_ref, vn_ref, cq_ref, ckc_ref, ckn_ref, o_ref,
                       m_ref, l_ref, acc_ref, s_ref, p_ref, *, tq, tkc, scale):
    j = pl.program_id(1)
    last = pl.num_programs(1) - 1

    @pl.when(j == 0)
    def _():
        m_ref[...] = jnp.full(m_ref.shape, NEG_INF, F32)
        l_ref[...] = jnp.zeros(l_ref.shape, F32)
        acc_ref[...] = jnp.zeros(acc_ref.shape, F32)

    cq = jnp.concatenate([cq_ref[0, :, h:h + 1] for h in range(H_FOX)], axis=0) * LOG2E

    def scores(h, k, ck, visible=None):
        q = (q_ref[:, h * DH_FOX:(h + 1) * DH_FOX] * (scale * LOG2E)).astype(BF16)
        s = lax.dot_general(q, k, _NT, preferred_element_type=F32) - ck[h:h + 1, :]
        if visible is not None:
            s = jnp.where(visible, s, NEG_INF)
        s_ref[h * tq:(h + 1) * tq, :s.shape[1]] = s

    def softmax_update(width):
        s = s_ref[:, :width]
        m_prev = m_ref[...]
        m_new = jnp.maximum(m_prev, jnp.max(s, axis=-1, keepdims=True) + cq)
        p = jnp.exp2(s - (m_new - cq))
        alpha = jnp.exp2(m_prev - m_new)
        l_ref[...] = alpha * l_ref[...] + jnp.sum(p, axis=-1, keepdims=True)
        m_ref[...] = m_new
        p_ref[:, :width] = p.astype(BF16)
        return alpha

    def values(h, v, alpha, width):
        rows = slice(h * tq, (h + 1) * tq)
        acc_ref[h] = alpha[rows] * acc_ref[h] + jnp.dot(p_ref[rows, :width], v, preferred_element_type=F32)

    ck = ckc_ref[0] * LOG2E
    for h in range(H_FOX):
        scores(h, kc_ref[pl.ds(h, tkc, stride=H_FOX), :].astype(BF16), ck)
    alpha = softmax_update(tkc)
    for h in range(H_FOX):
        values(h, vc_ref[pl.ds(h, tkc, stride=H_FOX), :].astype(BF16), alpha, tkc)

    @pl.when(j == last)
    def _():
        row = lax.broadcasted_iota(jnp.int32, (tq, tq), 0)
        col = lax.broadcasted_iota(jnp.int32, (tq, tq), 1)
        visible = col <= row
        ck = ckn_ref[0] * LOG2E
        for h in range(H_FOX):
            scores(h, kn_ref[:, h * DH_FOX:(h + 1) * DH_FOX].astype(BF16), ck, visible)
        alpha = softmax_update(tq)
        for h in range(H_FOX):
            cols = slice(h * DH_FOX, (h + 1) * DH_FOX)
            values(h, vn_ref[:, cols].astype(BF16), alpha, tq)
            o_ref[:, cols] = (acc_ref[h] / l_ref[h * tq:(h + 1) * tq]).astype(o_ref.dtype)


def _fox_sample(proj, cache_k, cache_v, c_col, c_row_cache, c_row_new, dest, *, layer, batch, seq,
                row_off, tkc, col_base):
    past = cache_k.shape[2] // H_FOX
    nkc = past // tkc
    off = row_off // seq
    width = H_FOX * DH_FOX
    qb, kb, vb = (col_base + n for n in range(3))
    cache_spec = pl.BlockSpec((None, None, tkc * H_FOX, DH_FOX), lambda b, j: (layer, b, j, 0))

    call = _call_into(
        functools.partial(_fox_sample_kernel, tq=seq, tkc=tkc, scale=DH_FOX ** -0.5), {0: dest}, 8,
        out_shape=jax.ShapeDtypeStruct((proj.shape[1], width), BF16),
        grid=(batch, nkc),
        in_specs=[
            pl.BlockSpec((None, seq, width), lambda b, j: (qb, off + b, 0)),
            cache_spec,
            cache_spec,
            pl.BlockSpec((None, seq, width), lambda b, j: (kb, off + b, 0)),
            pl.BlockSpec((None, seq, width), lambda b, j: (vb, off + b, 0)),
            pl.BlockSpec((1, seq, H_FOX), lambda b, j: (b, 0, 0)),
            pl.BlockSpec((1, H_FOX, tkc), lambda b, j: (b, 0, j)),
            pl.BlockSpec((1, H_FOX, seq), lambda b, j: (b, 0, 0)),
        ],
        out_specs=pl.BlockSpec((seq, width), lambda b, j: (off + b, 0)),
        scratch_shapes=[pltpu.VMEM((H_FOX * seq, 1), F32), pltpu.VMEM((H_FOX * seq, 1), F32),
                        pltpu.VMEM((H_FOX, seq, DH_FOX), F32),
                        pltpu.VMEM((H_FOX * seq, tkc), F32), pltpu.VMEM((H_FOX * seq, tkc), BF16)],
        compiler_params=_params(("parallel", "arbitrary"), 48),
        name="fox_sample",
    )
    return call(proj, cache_k, cache_v, proj, proj, c_col, c_row_cache, c_row_new)


def _out_proj_kernel(a_ref, b_ref, w_ref, x_ref, o_ref):
    ka = a_ref.shape[1]
    o_ref[...] = (x_ref[...]
                  + jnp.dot(a_ref[...], w_ref[:ka, :], preferred_element_type=F32)
                  + jnp.dot(b_ref[...], w_ref[ka:, :], preferred_element_type=F32))


def _out_proj(a, b, w, x, *, layer, tm):
    t, d = x.shape
    ka, kb = a.shape[1], b.shape[1]
    return pl.pallas_call(
        _out_proj_kernel,
        out_shape=jax.ShapeDtypeStruct((t, d), F32),
        grid=(t // tm,),
        in_specs=[
            pl.BlockSpec((tm, ka), lambda i: (i, 0)),
            pl.BlockSpec((tm, kb), lambda i: (i, 0)),
            pl.BlockSpec((None, ka + kb, d), lambda i: (layer, 0, 0)),
            pl.BlockSpec((tm, d), lambda i: (i, 0)),
        ],
        out_specs=pl.BlockSpec((tm, d), lambda i: (i, 0)),
        compiler_params=_params(("parallel",), 48),
        name="out_proj",
    )(a, b, w, x)


def _pool_kernel(x_ref, halo_ref, buf_ref, g_ref, w_ref, ps_ref, o_ref, bo_ref, full_ref, sum_a_ref,
                 sum_b_ref, *, tm, pos0):
    i = pl.program_id(1)
    g = g_ref[...]
    x = x_ref[...]
    u = _rms(x, g)
    hist = jnp.where(i == 0, buf_ref[0], _rms(halo_ref[...], g))
    lo, top = POOL_PAD, POOL_PAD + POOL_HALO
    end = top + tm
    for ref in (full_ref, sum_a_ref, sum_b_ref):
        ref[0:lo, :] = jnp.zeros((lo, x.shape[1]), F32)
    full_ref[lo:top, :] = hist
    full_ref[top:end, :] = u
    pos = pos0 + i * tm + lax.broadcasted_iota(jnp.int32, (tm, 1), 0)
    gc = x.shape[1] // len(POOL_WINDOWS)
    src, width = full_ref, 1
    for n, win in enumerate(POOL_WINDOWS):
        cols = slice(n * gc, (n + 1) * gc)
        if n == len(POOL_WINDOWS) - 1:
            total = src[top:end, cols] + src[top - width:end - width, cols]
        else:
            dst = (sum_a_ref, sum_b_ref)[n % 2]
            dst[lo:end, n * gc:] = src[lo:end, n * gc:] + src[lo - width:end - width, n * gc:]
            total = dst[top:end, cols]
            src = dst
        width = win
        inv_cnt = 1.0 / jnp.minimum(win, pos + 1).astype(F32)
        diff = (total * inv_cnt - u[:, cols]).astype(BF16)
        y = jnp.dot(diff, w_ref[n], preferred_element_type=F32) * ps_ref[:, cols]
        o_ref[:, cols] = x[:, cols] + y
    bo_ref[0] = u[tm - POOL_HALO:, :]


def _pool_mixer(x, buf, g, w, ps, dest, *, layer, mixer, batch, seq, row_off, tm, pos0):
    d = x.shape[1]
    nt = seq // tm
    off = row_off // tm
    per_halo = tm // POOL_HALO
    halo_off = row_off // POOL_HALO

    def halo_map(b, i):
        return (jnp.maximum(halo_off + (b * nt + i) * per_halo - 1, 0), 0)

    rows = pl.BlockSpec((tm, d), lambda b, i: (off + b * nt + i, 0))
    call = _call_into(
        functools.partial(_pool_kernel, tm=tm, pos0=pos0), {0: dest}, 6,
        out_shape=(jax.ShapeDtypeStruct(x.shape, F32),
                   jax.ShapeDtypeStruct((batch, POOL_HALO, d), F32)),
        grid=(batch, nt),
        in_specs=[
            rows,
            pl.BlockSpec((POOL_HALO, d), halo_map),
            pl.BlockSpec((1, POOL_HALO, d), lambda b, i: (b, 0, 0)),
            _layer_vec(layer, d),
            pl.BlockSpec((None,) + w.shape[1:], lambda b, i: (mixer, 0, 0, 0)),
            _layer_vec(mixer, d),
        ],
        out_specs=(rows, pl.BlockSpec((1, POOL_HALO, d), lambda b, i: (b, 0, 0))),
        scratch_shapes=[pltpu.VMEM((POOL_PAD + POOL_HALO + tm, d), F32)] * 3,
        compiler_params=_params(("parallel", "arbitrary"), 48),
        name="pool_mixer",
    )
    return call(x, x, buf, g, w, ps)


def _ffn_kernel(x_ref, g_ref, wg_ref, wu_ref, wd_ref, o_ref, h_ref):
    @pl.when(pl.program_id(1) == 0)
    def _():
        x = x_ref[...]
        h_ref[...] = _rms(x, g_ref[...]).astype(h_ref.dtype)
        o_ref[...] = x

    h = h_ref[...]
    a = jnp.dot(h, wg_ref[...].astype(BF16), preferred_element_type=F32)
    u = jnp.dot(h, wu_ref[...].astype(BF16), preferred_element_type=F32)
    act = (a * _sigmoid(a) * u).astype(BF16)
    o_ref[...] += jnp.dot(act, wd_ref[...].astype(BF16), preferred_element_type=F32)


def _ffn(x, g, wg, wu, wd, *, layer, tm, tf):
    t, d = x.shape
    f = wg.shape[2]
    return pl.pallas_call(
        _ffn_kernel,
        out_shape=jax.ShapeDtypeStruct((t, d), F32),
        grid=(t // tm, f // tf),
        in_specs=[
            pl.BlockSpec((tm, d), lambda i, j: (i, 0)),
            _layer_vec(layer, d),
            pl.BlockSpec((None, d, tf), lambda i, j: (layer, 0, j)),
            pl.BlockSpec((None, d, tf), lambda i, j: (layer, 0, j)),
            pl.BlockSpec((None, tf, d), lambda i, j: (layer, j, 0)),
        ],
        out_specs=pl.BlockSpec((tm, d), lambda i, j: (i, 0)),
        scratch_shapes=[pltpu.VMEM((tm, d), BF16)],
        compiler_params=_params(("parallel", "arbitrary"), 56),
        name="swiglu_ffn",
    )(x, g, wg, wu, wd)


def _ple_update(x_ref, g_ref, wg_ref, pp_ref, ps_ref, wp_ref, ntp):
    x = x_ref[...]
    h = _rms(x, g_ref[...]).astype(BF16)
    gate = _sigmoid(jnp.dot(h, wg_ref[...], preferred_element_type=F32))
    p = jnp.where(pl.program_id(0) < ntp, pp_ref[...], ps_ref[...])
    emb = jnp.dot(p.astype(BF16), wp_ref[...], preferred_element_type=F32)
    return x + gate * emb


def _ple_kernel(x_ref, g_ref, wg_ref, pp_ref, ps_ref, wp_ref, o_ref, *, ntp):
    o_ref[...] = _ple_update(x_ref, g_ref, wg_ref, pp_ref, ps_ref, wp_ref, ntp)


def _ple_final_kernel(x_ref, g_ref, wg_ref, pp_ref, ps_ref, wp_ref, gf_ref, yp_ref, ys_ref, *, ntp):
    y = _rms(_ple_update(x_ref, g_ref, wg_ref, pp_ref, ps_ref, wp_ref, ntp), gf_ref[...])

    @pl.when(pl.program_id(0) < ntp)
    def _():
        yp_ref[...] = y

    @pl.when(pl.program_id(0) >= ntp)
    def _():
        ys_ref[...] = y


def _ple(x, g, wg, p_prompt, p_sample, wp, final_g, *, layer, n_p, tm):
    t, d = x.shape
    pd = p_prompt.shape[2]
    ntp, nts = n_p // tm, (t - n_p) // tm
    rows = pl.BlockSpec((tm, d), lambda i: (i, 0))
    in_specs = [
        rows,
        _layer_vec(layer, d),
        pl.BlockSpec((None, d, d), lambda i: (layer, 0, 0)),
        pl.BlockSpec((None, tm, pd), lambda i: (layer, jnp.minimum(i, ntp - 1), 0)),
        pl.BlockSpec((None, tm, pd), lambda i: (layer, jnp.clip(i - ntp, 0, nts - 1), 0)),
        pl.BlockSpec((None, pd, d), lambda i: (layer, 0, 0)),
    ]
    if final_g is None:
        return pl.pallas_call(
            functools.partial(_ple_kernel, ntp=ntp),
            out_shape=jax.ShapeDtypeStruct((t, d), F32),
            grid=(t // tm,),
            in_specs=in_specs,
            out_specs=rows,
            compiler_params=_params(("parallel",), 56),
            name="gated_embedding",
        )(x, g, wg, p_prompt, p_sample, wp)
    return pl.pallas_call(
        functools.partial(_ple_final_kernel, ntp=ntp),
        out_shape=(jax.ShapeDtypeStruct((n_p, d), F32), jax.ShapeDtypeStruct((t - n_p, d), F32)),
        grid=(t // tm,),
        in_specs=in_specs + [pl.BlockSpec((1, d), lambda i: (0, 0))],
        out_specs=(pl.BlockSpec((tm, d), lambda i: (jnp.minimum(i, ntp - 1), 0)),
                   pl.BlockSpec((tm, d), lambda i: (jnp.clip(i - ntp, 0, nts - 1), 0))),
        compiler_params=_params(("arbitrary",), 56),
        name="gated_embedding_final_norm",
    )(x, g, wg, p_prompt, p_sample, wp, final_g)


def _rope_tables(pos0, seq):
    half = DK_RET // 2
    inv = ROPE_BASE ** (-jnp.arange(half, dtype=F32) / half)
    ang = (pos0 + jnp.arange(seq, dtype=jnp.int32)).astype(F32)[:, None] * inv[None, :]
    return jnp.cos(ang), jnp.sin(ang)


def _decay_tables():
    log_g = jnp.log1p(-jnp.power(2.0, -5.0 - jnp.arange(H_RET, dtype=F32)))
    idx = jnp.arange(CHUNK, dtype=F32)
    intra = jnp.exp(log_g[:, None, None] * jnp.abs(idx[:, None] - idx[None, :]))
    q_dec = jnp.exp(log_g[:, None] * (idx[None, :] + 1.0))[..., None]
    k_dec = jnp.exp(log_g[:, None] * (CHUNK - 1.0 - idx[None, :]))[..., None]
    s_dec = jnp.exp(log_g * CHUNK)[:, None, None]
    return intra, q_dec, k_dec, s_dec


def _tile(n, want):
    t = min(n, want)
    while n % t:
        t //= 2
    return t


def kernel(x_prompt, x_sample, p_prompt, p_sample, state_ret, cache_fox_k, cache_fox_v, cache_fox_logf,
           state_pool, norm_mix, w_in, b_forget, w_out, w_pool, pool_scale, norm_ffn, w_gate, w_up, w_down,
           norm_ple, ple_gate, ple_proj, final_norm):
    bp, tp, d = x_prompt.shape
    bs, ts, _ = x_sample.shape
    depth = norm_mix.shape[0]
    past = cache_fox_k.shape[2]
    n_p, n_s = bp * tp, bs * ts
    n_tok = n_p + n_s
    ret_w = H_RET * DK_RET
    fox_w = H_FOX * DH_FOX
    main_w = 4 * ret_w + 3 * fox_w
    assert ret_w == fox_w
    fox_base = (4 * ret_w) // fox_w

    x = jnp.concatenate([x_prompt.reshape(n_p, d), x_sample.reshape(n_s, d)], axis=0)
    p_prompt = p_prompt.reshape(depth, n_p, -1)
    p_sample = p_sample.reshape(depth, n_s, -1)
    tm = _tile(n_tok, 512)

    as_rows = lambda a: a.reshape(a.shape[0], 1, a.shape[1])
    norm_mix, norm_ffn, norm_ple, pool_scale = map(as_rows, (norm_mix, norm_ffn, norm_ple, pool_scale))
    w_out_b, w_pool_b = w_out.astype(BF16), w_pool.astype(BF16)
    w_in_b = jnp.transpose(w_in[:, :, :main_w].reshape(w_in.shape[0], d, main_w // fox_w, fox_w),
                           (0, 2, 1, 3)).astype(BF16)
    ple_gate_b, ple_proj_b = ple_gate.astype(BF16), ple_proj.astype(BF16)
    w_f = jnp.pad(w_in[:, :, main_w:], ((0, 0), (0, 0), (0, LANES - H_FOX))).astype(BF16)
    b_f = jnp.pad(b_forget.astype(F32), ((0, 0), (0, LANES - H_FOX)))[:, None, :]
    cache_k = cache_fox_k.reshape(cache_fox_k.shape[:2] + (past * H_FOX, DH_FOX))
    cache_v = cache_fox_v.reshape(cache_fox_v.shape[:2] + (past * H_FOX, DH_FOX))

    decay = _decay_tables()
    rope_p = _rope_tables(0, tp)
    rope_s = _rope_tables(past, ts)
    zero_state = jnp.zeros((1, bp) + state_ret.shape[2:], F32)
    zero_buf = jnp.zeros((bp, POOL_HALO, d), F32)

    n_even = w_in.shape[0]
    kv_new = (None,) * 4
    ret_p = ret_s = None
    new_lf, new_pool_p, new_pool_s = [], [], []
    for i in range(depth):
        if i % 2 == 0:
            e = i // 2
            proj, lf, *kv_new = _in_proj(x, norm_mix, w_in_b, w_f, b_f, kv_new, layer=i, wlayer=e,
                                         n_layers=n_even, n_p=n_p, k_tile=fox_base + 1,
                                         tm=_tile(math.gcd(n_p, n_s), 512))
            lf = lf[:, :H_FOX]
            lf_p = lf[:n_p].reshape(bp, tp, H_FOX)
            lf_s = lf[n_p:].reshape(bs, ts, H_FOX)
            c_row_p = _cumsum_rows(jnp.transpose(lf_p, (0, 2, 1)))
            lf_all_s = jnp.concatenate([cache_fox_logf[e].astype(F32), lf_s], axis=1)
            c_row_s = _cumsum_rows(jnp.transpose(lf_all_s, (0, 2, 1)))
            c_col_p = jnp.transpose(c_row_p, (0, 2, 1))
            c_col_s = jnp.transpose(c_row_s[:, :, past:], (0, 2, 1))

            o_r, ret_p = _retention(proj, *rope_p, decay, zero_state, None, ret_p, layer=0, out_layer=e,
                                    n_layers=n_even, batch=bp, seq=tp, row_off=0,
                                    chunks=_tile(tp, 512) // CHUNK, heads=2)
            o_r, ret_s = _retention(proj, *rope_s, decay, state_ret.astype(F32), o_r, ret_s, layer=e,
                                    out_layer=e, n_layers=n_even, batch=bs, seq=ts, row_off=n_p,
                                    chunks=ts // CHUNK, heads=H_RET)
            o_f = _fox_prompt(proj, c_col_p, c_row_p, batch=bp, seq=tp, row_off=0, tq=_tile(tp, 512),
                              col_base=fox_base)
            o_f = _fox_sample(proj, cache_k, cache_v, c_col_s, c_row_s, c_row_s[:, :, past:], o_f,
                              layer=e, batch=bs, seq=ts, row_off=n_p, tkc=_tile(past, 1024),
                              col_base=fox_base)
            x = _out_proj(o_r, o_f, w_out_b, x, layer=e, tm=tm)

            new_lf.append((lf_p, lf_s))
        else:
            o = i // 2
            buf_s = jnp.pad(state_pool[o].astype(F32), ((0, 0), (1, 0), (0, 0)))
            x_new, pool_p = _pool_mixer(x, zero_buf, norm_mix, w_pool_b, pool_scale, None, layer=i, mixer=o,
                                        batch=bp, seq=tp, row_off=0, tm=_tile(tp, 512), pos0=0)
            x, pool_s = _pool_mixer(x, buf_s, norm_mix, w_pool_b, pool_scale, x_new, layer=i, mixer=o,
                                    batch=bs, seq=ts, row_off=n_p, tm=ts, pos0=past)
            new_pool_p.append(pool_p[:, 1:])
            new_pool_s.append(pool_s[:, 1:])
        x = _ffn(x, norm_ffn, w_gate, w_up, w_down, layer=i, tm=_tile(n_tok, 1024), tf=256)
        x = _ple(x, norm_ple, ple_gate_b, p_prompt, p_sample, ple_proj_b,
                 final_norm[None, :] if i == depth - 1 else None, layer=i, n_p=n_p,
                 tm=_tile(math.gcd(n_p, n_s), 256))
    y_p, y_s = x
    k_p, k_s, v_p, v_s = kv_new
    prompt_kv = (n_even, bp, tp, H_FOX, DH_FOX)
    sample_kv = (n_even, bs, ts, H_FOX, DH_FOX)
    return (
        y_p.reshape(bp, tp, d),
        y_s.reshape(bs, ts, d),
        ret_p,
        ret_s.astype(state_ret.dtype),
        k_p.reshape(prompt_kv),
        k_s.reshape(sample_kv),
        v_p.reshape(prompt_kv),
        v_s.reshape(sample_kv),
        jnp.stack([lf[0] for lf in new_lf]),
        jnp.stack([lf[1] for lf in new_lf]),
        jnp.stack(new_pool_p),
        jnp.stack(new_pool_s),
    )
```

```python
import functools
import math

import jax
import jax.numpy as jnp
from jax import lax
from jax.experimental import pallas as pl
from jax.experimental.pallas import tpu as pltpu

F32 = jnp.float32
BF16 = jnp.bfloat16

EPS = 1e-6
NEG_INF = -1e30
ROPE_BASE = 10000.0
LOG2E = math.log2(math.e)
CHUNK = 64
H_RET = 4
DK_RET = 256
H_FOX = 8
DH_FOX = 128
POOL_WINDOWS = (2, 4, 8, 16)
POOL_HALO = 16
POOL_PAD = 8
assert POOL_WINDOWS == tuple(2 ** (n + 1) for n in range(len(POOL_WINDOWS))) and POOL_WINDOWS[-2] <= POOL_PAD
LANES = 128
CUMSUM_CHUNK = 256

_NT = (((1,), (1,)), ((), ()))
_TN = (((0,), (0,)), ((), ()))


def _params(semantics, vmem_mib):
    return pltpu.CompilerParams(dimension_semantics=semantics, vmem_limit_bytes=vmem_mib << 20)


def _rms(x, g):
    return x * lax.rsqrt(jnp.mean(x * x, axis=-1, keepdims=True) + EPS) * g


def _sigmoid(x):
    return 1.0 / (1.0 + jnp.exp(-x))


def _layer_vec(layer, d):
    return pl.BlockSpec((None, 1, d), lambda *_: (layer, 0, 0))


def _call_into(kernel, dests, n_in, **kwargs):
    dests = {k: a for k, a in dests.items() if a is not None}
    if not dests:
        return pl.pallas_call(kernel, **kwargs)
    n_d = len(dests)

    def body(*refs):
        kernel(*refs[:n_in], *refs[n_in + n_d:])

    kwargs["in_specs"] = list(kwargs["in_specs"]) + [pl.BlockSpec(memory_space=pl.ANY)] * n_d
    aliases = {n_in + pos: out for pos, out in enumerate(dests)}
    call = pl.pallas_call(body, input_output_aliases=aliases, **kwargs)
    return lambda *args: call(*args, *dests.values())


def _in_proj_kernel(xp_ref, xs_ref, g_ref, w_ref, wf_ref, bf_ref, o_ref, lf_ref, kp_ref, ks_ref, vp_ref,
                    vs_ref, h_ref, *, tm, jk, ntp):
    i = pl.program_id(0)
    j = pl.program_id(1)

    @pl.when(j == 0)
    def _():
        x = jnp.where(i < ntp, xp_ref[...], xs_ref[...])
        h = _rms(x, g_ref[...]).astype(h_ref.dtype)
        h_ref[...] = h
        z = jnp.dot(h, wf_ref[...], preferred_element_type=F32) + bf_ref[...]
        lf_ref[...] = jnp.minimum(z, 0.0) - jnp.log1p(jnp.exp(-jnp.abs(z)))

    o_ref[...] = jnp.dot(h_ref[...], w_ref[...], preferred_element_type=F32)

    def to_head_rows(dst_ref):
        for hd in range(H_FOX):
            dst_ref[pl.ds(hd, tm, stride=H_FOX), :] = o_ref[:, hd * DH_FOX:(hd + 1) * DH_FOX]

    for jj, prompt_ref, sample_ref in ((jk, kp_ref, ks_ref), (jk + 1, vp_ref, vs_ref)):
        @pl.when((j == jj) & (i < ntp))
        def _():
            to_head_rows(prompt_ref)

        @pl.when((j == jj) & (i >= ntp))
        def _():
            to_head_rows(sample_ref)


def _two_group_rows(block, ntp, nts, sample_first):
    return (pl.BlockSpec(block, lambda i, *_: (jnp.minimum(i, ntp - 1), 0)),
            pl.BlockSpec(block, lambda i, *_: (sample_first + jnp.clip(i - ntp, 0, nts - 1), 0)))


def _in_proj(x_src, g, w, wf, bf, kv_dests, *, layer, wlayer, n_layers, n, n_p, n_s, k_col, tm):
    xp, xs, xs_off = x_src
    d = xp.shape[1]
    t = n_p + n_s
    tn = H_FOX * DH_FOX
    ntp, nts = n_p // tm, n_s // tm
    rows = (tm * H_FOX, DH_FOX)
    prompt_rows = pl.BlockSpec(rows, lambda i, j: (wlayer * ntp + jnp.minimum(i, ntp - 1), 0))
    sample_rows = pl.BlockSpec(rows, lambda i, j: (wlayer * nts + jnp.clip(i - ntp, 0, nts - 1), 0))
    kv_p = jax.ShapeDtypeStruct((n_layers * n_p * H_FOX, DH_FOX), F32)
    kv_s = jax.ShapeDtypeStruct((n_layers * n_s * H_FOX, DH_FOX), F32)
    lanes = wf.shape[2]
    call = _call_into(
        functools.partial(_in_proj_kernel, tm=tm, jk=k_col // tn, ntp=ntp),
        {2 + k: a for k, a in enumerate(kv_dests)}, 6,
        out_shape=(jax.ShapeDtypeStruct((t, n), F32), jax.ShapeDtypeStruct((t, lanes), F32),
                   kv_p, kv_s, kv_p, kv_s),
        grid=(t // tm, n // tn),
        in_specs=[
            *_two_group_rows((tm, d), ntp, nts, xs_off // tm),
            _layer_vec(layer, d),
            pl.BlockSpec((None, d, tn), lambda i, j: (wlayer, 0, j)),
            pl.BlockSpec((None, d, lanes), lambda i, j: (wlayer, 0, 0)),
            pl.BlockSpec((None, 1, lanes), lambda i, j: (wlayer, 0, 0)),
        ],
        out_specs=(pl.BlockSpec((tm, tn), lambda i, j: (i, j)),
                   pl.BlockSpec((tm, lanes), lambda i, j: (i, 0)),
                   prompt_rows, sample_rows, prompt_rows, sample_rows),
        scratch_shapes=[pltpu.VMEM((tm, d), BF16)],
        compiler_params=_params(("arbitrary", "arbitrary"), 56),
        name="in_proj",
    )
    return call(xp, xs, g, w, wf, bf)


def _cumsum_kernel(lf_ref, c_ref, *, tk):
    r = lax.broadcasted_iota(jnp.int32, (CUMSUM_CHUNK, CUMSUM_CHUNK), 0)
    c = lax.broadcasted_iota(jnp.int32, (CUMSUM_CHUNK, CUMSUM_CHUNK), 1)
    upper = (r <= c).astype(F32)
    for b in range(lf_ref.shape[0]):
        carry = jnp.zeros((H_FOX, 1), F32)
        for s in range(0, tk, CUMSUM_CHUNK):
            w = min(CUMSUM_CHUNK, tk - s)
            blk = lf_ref[b, :, s:s + w]
            cs = jnp.dot(blk, upper[:w, :w], precision=lax.Precision.HIGHEST,
                         preferred_element_type=F32) + carry
            c_ref[b, :, s:s + w] = cs
            carry = cs[:, w - 1:w]


def _cumsum_rows(lf_rows):
    b, h, tk = lf_rows.shape
    bb = math.gcd(b, 8)
    return pl.pallas_call(
        functools.partial(_cumsum_kernel, tk=tk),
        out_shape=jax.ShapeDtypeStruct((b, h, tk), F32),
        grid=(b // bb,),
        in_specs=[pl.BlockSpec((bb, h, tk), lambda i: (i, 0, 0))],
        out_specs=pl.BlockSpec((bb, h, tk), lambda i: (i, 0, 0)),
        compiler_params=_params(("parallel",), 32),
        name="logf_cumsum",
    )(lf_rows)


def _rope(x, cos, sin):
    half = x.shape[-1] // 2
    x1, x2 = x[:, :half], x[:, half:]
    return jnp.concatenate([x1 * cos - x2 * sin, x1 * sin + x2 * cos], axis=-1)


def _retention_kernel(q_ref, k_ref, v_ref, g_ref, cos_ref, sin_ref, intra_ref, qdec_ref, kdec_ref,
                      sdec_ref, s0_ref, o_ref, s_out_ref, state_ref, *, chunks, heads, scale):
    i = pl.program_id(2)

    @pl.when(i == 0)
    def _():
        state_ref[...] = s0_ref[0]

    for c in range(chunks):
        rows = slice(c * CHUNK, (c + 1) * CHUNK)
        cos = cos_ref[rows, :]
        sin = sin_ref[rows, :]
        for h in range(heads):
            cols = slice(h * DK_RET, (h + 1) * DK_RET)
            q = _rope(q_ref[rows, cols], cos, sin)
            k = _rope(k_ref[rows, cols], cos, sin) * scale
            qb = q.astype(BF16)
            vb = v_ref[rows, cols].astype(BF16)
            state = state_ref[h]
            sc = lax.dot_general(qb, k.astype(BF16), _NT, preferred_element_type=F32) * intra_ref[h]
            o = (jnp.dot(sc.astype(BF16), vb, preferred_element_type=F32)
                 + jnp.dot(qb, state.astype(BF16), preferred_element_type=F32) * qdec_ref[h])
            kd = (k * kdec_ref[h]).astype(BF16)
            state_ref[h] = state * sdec_ref[h] + lax.dot_general(kd, vb, _TN, preferred_element_type=F32)
            o = o * lax.rsqrt(jnp.mean(o * o, axis=-1, keepdims=True) + EPS)
            gate = g_ref[rows, cols]
            o_ref[rows, cols] = (o * (gate * _sigmoid(gate))).astype(o_ref.dtype)

    @pl.when(i == pl.num_programs(2) - 1)
    def _():
        s_out_ref[0] = state_ref[...]


def _retention(proj, cos, sin, decay, state, dest, state_dest, *, layer, out_layer, n_layers, batch, seq,
               row_off, chunks, heads):
    tblk = chunks * CHUNK
    nblk = seq // tblk
    off = row_off // tblk
    intra, q_dec, k_dec, s_dec = decay

    def col(base):
        return pl.BlockSpec((tblk, heads * DK_RET),
                            lambda b, h, i: (off + b * nblk + i, base // heads + h))

    per_head = lambda shape: pl.BlockSpec((heads,) + shape, lambda b, h, i: (h, 0, 0))
    table = pl.BlockSpec((tblk, DK_RET // 2), lambda b, h, i: (i, 0))
    state_block = (None, 1, heads, DK_RET, DK_RET)
    call = _call_into(
        functools.partial(_retention_kernel, chunks=chunks, heads=heads, scale=DK_RET ** -0.5),
        {0: dest, 1: state_dest}, 11,
        out_shape=(jax.ShapeDtypeStruct((proj.shape[0], H_RET * DK_RET), BF16),
                   jax.ShapeDtypeStruct((n_layers, batch, H_RET, DK_RET, DK_RET), F32)),
        grid=(batch, H_RET // heads, nblk),
        in_specs=[col(0), col(H_RET), col(2 * H_RET), col(3 * H_RET), table, table,
                  per_head((CHUNK, CHUNK)), per_head((CHUNK, 1)), per_head((CHUNK, 1)),
                  per_head((1, 1)),
                  pl.BlockSpec(state_block, lambda b, h, i: (layer, b, h, 0, 0))],
        out_specs=(col(0), pl.BlockSpec(state_block, lambda b, h, i: (out_layer, b, h, 0, 0))),
        scratch_shapes=[pltpu.VMEM((heads, DK_RET, DK_RET), F32)],
        compiler_params=_params(("parallel", "parallel", "arbitrary"), 32),
        name="retention",
    )
    return call(proj, proj, proj, proj, cos, sin, intra, q_dec, k_dec, s_dec, state)


def _fox_prompt_kernel(qi_ref, kj_ref, q_ref, k_ref, v_ref, cq_ref, ck_ref, o_ref, m_ref, l_ref, acc_ref, *,
                       tq, scale):
    qi = qi_ref[pl.program_id(1)]
    kj = kj_ref[pl.program_id(1)]

    @pl.when(kj == 0)
    def _():
        m_ref[...] = jnp.full(m_ref.shape, NEG_INF, F32)
        l_ref[...] = jnp.zeros(l_ref.shape, F32)
        acc_ref[...] = jnp.zeros(acc_ref.shape, F32)

    def step(masked):
        if masked:
            key = lax.broadcasted_iota(jnp.int32, (tq, tq), 0)
            qry = lax.broadcasted_iota(jnp.int32, (tq, tq), 1)
            visible = key <= qry
        for h in range(H_FOX):
            cols = slice(h * DH_FOX, (h + 1) * DH_FOX)
            q = (q_ref[:, cols] * (scale * LOG2E)).astype(BF16)
            k = k_ref[:, cols].astype(BF16)
            cq = cq_ref[0, h:h + 1, :] * LOG2E
            s = lax.dot_general(k, q, _NT, preferred_element_type=F32) - ck_ref[0, :, h:h + 1] * LOG2E
            if masked:
                s = jnp.where(visible, s, NEG_INF)
            m_prev = m_ref[h]
            m_new = jnp.maximum(m_prev, jnp.max(s, axis=0, keepdims=True) + cq)
            p = jnp.exp2(s - (m_new - cq))
            alpha = jnp.exp2(m_prev - m_new)
            l_ref[h] = alpha * l_ref[h] + jnp.sum(p, axis=0, keepdims=True)
            pv = lax.dot_general(v_ref[:, cols].astype(BF16), p.astype(BF16), _TN,
                                 preferred_element_type=F32)
            acc_ref[h] = alpha * acc_ref[h] + pv
            m_ref[h] = m_new

    @pl.when(kj < qi)
    def _():
        step(False)

    @pl.when(kj == qi)
    def _():
        step(True)
        for h in range(H_FOX):
            o = (acc_ref[h] / l_ref[h]).T
            o_ref[:, h * DH_FOX:(h + 1) * DH_FOX] = o.astype(o_ref.dtype)


def _fox_prompt(proj, c_col, c_row, *, batch, seq, row_off, tq, col_base):
    nq = seq // tq
    off = row_off // tq
    width = H_FOX * DH_FOX
    qb, kb, vb = (col_base + n for n in range(3))

    pairs = [(i, j) for i in range(nq) for j in range(i + 1)]
    qi_tab = jnp.asarray([i for i, _ in pairs], jnp.int32)
    kj_tab = jnp.asarray([j for _, j in pairs], jnp.int32)
    grid_spec = pltpu.PrefetchScalarGridSpec(
        num_scalar_prefetch=2,
        grid=(batch, len(pairs)),
        in_specs=[
            pl.BlockSpec((tq, width), lambda b, s, qi, kj: (off + b * nq + qi[s], qb)),
            pl.BlockSpec((tq, width), lambda b, s, qi, kj: (off + b * nq + kj[s], kb)),
            pl.BlockSpec((tq, width), lambda b, s, qi, kj: (off + b * nq + kj[s], vb)),
            pl.BlockSpec((1, H_FOX, tq), lambda b, s, qi, kj: (b, 0, qi[s])),
            pl.BlockSpec((1, tq, H_FOX), lambda b, s, qi, kj: (b, kj[s], 0)),
        ],
        out_specs=pl.BlockSpec((tq, width), lambda b, s, qi, kj: (off + b * nq + qi[s], 0)),
        scratch_shapes=[pltpu.VMEM((H_FOX, 1, tq), F32), pltpu.VMEM((H_FOX, 1, tq), F32),
                        pltpu.VMEM((H_FOX, DH_FOX, tq), F32)],
    )
    return pl.pallas_call(
        functools.partial(_fox_prompt_kernel, tq=tq, scale=DH_FOX ** -0.5),
        out_shape=jax.ShapeDtypeStruct((proj.shape[0], width), BF16),
        grid_spec=grid_spec,
        compiler_params=_params(("parallel", "arbitrary"), 48),
        name="fox_prompt",
    )(qi_tab, kj_tab, proj, proj, proj, c_row, c_col)


def _fox_sample_kernel(q_ref, kc_ref, vc_ref, kn_ref, vn_ref, cq_ref, ckc_ref, ckn_ref, o_ref,
                       m_ref, l_ref, acc_ref, s_ref, p_ref, *, tq, tkc, scale):
    j = pl.program_id(1)
    last = pl.num_programs(1) - 1

    @pl.when(j == 0)
    def _():
        m_ref[...] = jnp.full(m_ref.shape, NEG_INF, F32)
        l_ref[...] = jnp.zeros(l_ref.shape, F32)
        acc_ref[...] = jnp.zeros(acc_ref.shape, F32)

    cq = jnp.concatenate([cq_ref[0, :, h:h + 1] for h in range(H_FOX)], axis=0) * LOG2E

    def scores(h, k, ck, visible=None):
        q = (q_ref[:, h * DH_FOX:(h + 1) * DH_FOX] * (scale * LOG2E)).astype(BF16)
        s = lax.dot_general(q, k, _NT, preferred_element_type=F32) - ck[h:h + 1, :]
        if visible is not None:
            s = jnp.where(visible, s, NEG_INF)
        s_ref[h * tq:(h + 1) * tq, :s.shape[1]] = s

    def softmax_update(width):
        s = s_ref[:, :width]
        m_prev = m_ref[...]
        m_new = jnp.maximum(m_prev, jnp.max(s, axis=-1, keepdims=True) + cq)
        p = jnp.exp2(s - (m_new - cq))
        alpha = jnp.exp2(m_prev - m_new)
        l_ref[...] = alpha * l_ref[...] + jnp.sum(p, axis=-1, keepdims=True)
        m_ref[...] = m_new
        p_ref[:, :width] = p.astype(BF16)
        return alpha

    def values(h, v, alpha, width):
        rows = slice(h * tq, (h + 1) * tq)
        acc_ref[h] = alpha[rows] * acc_ref[h] + jnp.dot(p_ref[rows, :width], v, preferred_element_type=F32)

    ck = ckc_ref[0] * LOG2E
    for h in range(H_FOX):
        scores(h, kc_ref[pl.ds(h, tkc, stride=H_FOX), :].astype(BF16), ck)
    alpha = softmax_update(tkc)
    for h in range(H_FOX):
        values(h, vc_ref[pl.ds(h, tkc, stride=H_FOX), :].astype(BF16), alpha, tkc)

    @pl.when(j == last)
    def _():
        row = lax.broadcasted_iota(jnp.int32, (tq, tq), 0)
        col = lax.broadcasted_iota(jnp.int32, (tq, tq), 1)
        visible = col <= row
        ck = ckn_ref[0] * LOG2E
        for h in range(H_FOX):
            scores(h, kn_ref[:, h * DH_FOX:(h + 1) * DH_FOX].astype(BF16), ck, visible)
        alpha = softmax_update(tq)
        for h in range(H_FOX):
            cols = slice(h * DH_FOX, (h + 1) * DH_FOX)
            values(h, vn_ref[:, cols].astype(BF16), alpha, tq)
            o_ref[:, cols] = (acc_ref[h] / l_ref[h * tq:(h + 1) * tq]).astype(o_ref.dtype)


def _fox_sample(proj, cache_k, cache_v, c_col, c_row_cache, c_row_new, dest, *, layer, batch, seq,
                row_off, tkc, col_base):
    past = cache_k.shape[2] // H_FOX
    nkc = past // tkc
    off = row_off // seq
    width = H_FOX * DH_FOX
    qb, kb, vb = (col_base + n for n in range(3))
    cache_spec = pl.BlockSpec((None, None, tkc * H_FOX, DH_FOX), lambda b, j: (layer, b, j, 0))

    call = _call_into(
        functools.partial(_fox_sample_kernel, tq=seq, tkc=tkc, scale=DH_FOX ** -0.5), {0: dest}, 8,
        out_shape=jax.ShapeDtypeStruct((proj.shape[0], width), BF16),
        grid=(batch, nkc),
        in_specs=[
            pl.BlockSpec((seq, width), lambda b, j: (off + b, qb)),
            cache_spec,
            cache_spec,
            pl.BlockSpec((seq, width), lambda b, j: (off + b, kb)),
            pl.BlockSpec((seq, width), lambda b, j: (off + b, vb)),
            pl.BlockSpec((1, seq, H_FOX), lambda b, j: (b, 0, 0)),
            pl.BlockSpec((1, H_FOX, tkc), lambda b, j: (b, 0, j)),
            pl.BlockSpec((1, H_FOX, seq), lambda b, j: (b, 0, 0)),
        ],
        out_specs=pl.BlockSpec((seq, width), lambda b, j: (off + b, 0)),
        scratch_shapes=[pltpu.VMEM((H_FOX * seq, 1), F32), pltpu.VMEM((H_FOX * seq, 1), F32),
                        pltpu.VMEM((H_FOX, seq, DH_FOX), F32),
                        pltpu.VMEM((H_FOX * seq, tkc), F32), pltpu.VMEM((H_FOX * seq, tkc), BF16)],
        compiler_params=_params(("parallel", "arbitrary"), 48),
        name="fox_sample",
    )
    return call(proj, cache_k, cache_v, proj, proj, c_col, c_row_cache, c_row_new)


def _out_proj_kernel(a_ref, b_ref, w_ref, xp_ref, xs_ref, o_ref, *, ntp):
    ka = a_ref.shape[1]
    o_ref[...] = (jnp.where(pl.program_id(0) < ntp, xp_ref[...], xs_ref[...])
                  + jnp.dot(a_ref[...], w_ref[:ka, :], preferred_element_type=F32)
                  + jnp.dot(b_ref[...], w_ref[ka:, :], preferred_element_type=F32))


def _out_proj(a, b, w, x_src, *, layer, n_p, tm):
    xp, xs, xs_off = x_src
    t, d = a.shape[0], xp.shape[1]
    ka, kb = a.shape[1], b.shape[1]
    ntp = n_p // tm
    return pl.pallas_call(
        functools.partial(_out_proj_kernel, ntp=ntp),
        out_shape=jax.ShapeDtypeStruct((t, d), F32),
        grid=(t // tm,),
        in_specs=[
            pl.BlockSpec((tm, ka), lambda i: (i, 0)),
            pl.BlockSpec((tm, kb), lambda i: (i, 0)),
            pl.BlockSpec((None, ka + kb, d), lambda i: (layer, 0, 0)),
            *_two_group_rows((tm, d), ntp, (t - n_p) // tm, xs_off // tm),
        ],
        out_specs=pl.BlockSpec((tm, d), lambda i: (i, 0)),
        compiler_params=_params(("parallel",), 56),
        name="out_proj",
    )(a, b, w, xp, xs)


def _pool_kernel(x_ref, halo_ref, buf_ref, g_ref, w_ref, ps_ref, o_ref, bo_ref, full_ref, sum_a_ref,
                 sum_b_ref, *, tm, pos0):
    i = pl.program_id(1)
    g = g_ref[...]
    x = x_ref[...]
    u = _rms(x, g)
    hist = jnp.where(i == 0, buf_ref[0], _rms(halo_ref[...], g))
    lo, top = POOL_PAD, POOL_PAD + POOL_HALO
    end = top + tm
    for ref in (full_ref, sum_a_ref, sum_b_ref):
        ref[0:lo, :] = jnp.zeros((lo, x.shape[1]), F32)
    full_ref[lo:top, :] = hist
    full_ref[top:end, :] = u
    pos = pos0 + i * tm + lax.broadcasted_iota(jnp.int32, (tm, 1), 0)
    gc = x.shape[1] // len(POOL_WINDOWS)
    src, width = full_ref, 1
    for n, win in enumerate(POOL_WINDOWS):
        cols = slice(n * gc, (n + 1) * gc)
        if n == len(POOL_WINDOWS) - 1:
            total = src[top:end, cols] + src[top - width:end - width, cols]
        else:
            dst = (sum_a_ref, sum_b_ref)[n % 2]
            dst[lo:end, n * gc:] = src[lo:end, n * gc:] + src[lo - width:end - width, n * gc:]
            total = dst[top:end, cols]
            src = dst
        width = win
        inv_cnt = 1.0 / jnp.minimum(win, pos + 1).astype(F32)
        diff = (total * inv_cnt - u[:, cols]).astype(BF16)
        y = jnp.dot(diff, w_ref[n], preferred_element_type=F32) * ps_ref[:, cols]
        o_ref[:, cols] = x[:, cols] + y
    bo_ref[0] = u[tm - POOL_HALO:, :]


def _pool_mixer(x, buf, g, w, ps, dest, *, layer, mixer, batch, seq, row_off, tm, pos0):
    d = x.shape[1]
    nt = seq // tm
    off = row_off // tm
    per_halo = tm // POOL_HALO
    halo_off = row_off // POOL_HALO

    def halo_map(b, i):
        return (jnp.maximum(halo_off + (b * nt + i) * per_halo - 1, 0), 0)

    rows = pl.BlockSpec((tm, d), lambda b, i: (off + b * nt + i, 0))
    call = _call_into(
        functools.partial(_pool_kernel, tm=tm, pos0=pos0), {0: dest}, 6,
        out_shape=(jax.ShapeDtypeStruct(x.shape, F32),
                   jax.ShapeDtypeStruct((batch, POOL_HALO, d), F32)),
        grid=(batch, nt),
        in_specs=[
            rows,
            pl.BlockSpec((POOL_HALO, d), halo_map),
            pl.BlockSpec((1, POOL_HALO, d), lambda b, i: (b, 0, 0)),
            _layer_vec(layer, d),
            pl.BlockSpec((None,) + w.shape[1:], lambda b, i: (mixer, 0, 0, 0)),
            _layer_vec(mixer, d),
        ],
        out_specs=(rows, pl.BlockSpec((1, POOL_HALO, d), lambda b, i: (b, 0, 0))),
        scratch_shapes=[pltpu.VMEM((POOL_PAD + POOL_HALO + tm, d), F32)] * 3,
        compiler_params=_params(("parallel", "arbitrary"), 48),
        name="pool_mixer",
    )
    return call(x, x, buf, g, w, ps)


def _ffn_kernel(x_ref, g_ref, wg_ref, wu_ref, wd_ref, o_ref, h_ref):
    @pl.when(pl.program_id(1) == 0)
    def _():
        x = x_ref[...]
        h_ref[...] = _rms(x, g_ref[...]).astype(h_ref.dtype)
        o_ref[...] = x

    h = h_ref[...]
    a = jnp.dot(h, wg_ref[...].astype(BF16), preferred_element_type=F32)
    u = jnp.dot(h, wu_ref[...].astype(BF16), preferred_element_type=F32)
    act = (a * _sigmoid(a) * u).astype(BF16)
    o_ref[...] += jnp.dot(act, wd_ref[...].astype(BF16), preferred_element_type=F32)


def _ffn(x, g, wg, wu, wd, *, layer, tm, tf):
    t, d = x.shape
    f = wg.shape[2]
    return pl.pallas_call(
        _ffn_kernel,
        out_shape=jax.ShapeDtypeStruct((t, d), F32),
        grid=(t // tm, f // tf),
        in_specs=[
            pl.BlockSpec((tm, d), lambda i, j: (i, 0)),
            _layer_vec(layer, d),
            pl.BlockSpec((None, d, tf), lambda i, j: (layer, 0, j)),
            pl.BlockSpec((None, d, tf), lambda i, j: (layer, 0, j)),
            pl.BlockSpec((None, tf, d), lambda i, j: (layer, j, 0)),
        ],
        out_specs=pl.BlockSpec((tm, d), lambda i, j: (i, 0)),
        scratch_shapes=[pltpu.VMEM((tm, d), BF16)],
        compiler_params=_params(("parallel", "arbitrary"), 56),
        name="swiglu_ffn",
    )(x, g, wg, wu, wd)


def _ple_update(x_ref, g_ref, wg_ref, pp_ref, ps_ref, wp_ref, ntp):
    x = x_ref[...]
    h = _rms(x, g_ref[...]).astype(BF16)
    gate = _sigmoid(jnp.dot(h, wg_ref[...], preferred_element_type=F32))
    p = jnp.where(pl.program_id(0) < ntp, pp_ref[...], ps_ref[...])
    emb = jnp.dot(p.astype(BF16), wp_ref[...], preferred_element_type=F32)
    return x + gate * emb


def _ple_kernel(x_ref, g_ref, wg_ref, pp_ref, ps_ref, wp_ref, o_ref, *, ntp):
    o_ref[...] = _ple_update(x_ref, g_ref, wg_ref, pp_ref, ps_ref, wp_ref, ntp)


def _ple_final_kernel(x_ref, g_ref, wg_ref, pp_ref, ps_ref, wp_ref, gf_ref, yp_ref, ys_ref, *, ntp):
    y = _rms(_ple_update(x_ref, g_ref, wg_ref, pp_ref, ps_ref, wp_ref, ntp), gf_ref[...])

    @pl.when(pl.program_id(0) < ntp)
    def _():
        yp_ref[...] = y

    @pl.when(pl.program_id(0) >= ntp)
    def _():
        ys_ref[...] = y


def _ple(x, g, wg, p_prompt, p_sample, wp, final_g, *, layer, n_p, tm):
    t, d = x.shape
    pd = p_prompt.shape[2]
    ntp, nts = n_p // tm, (t - n_p) // tm
    rows = pl.BlockSpec((tm, d), lambda i: (i, 0))
    in_specs = [
        rows,
        _layer_vec(layer, d),
        pl.BlockSpec((None, d, d), lambda i: (layer, 0, 0)),
        pl.BlockSpec((None, tm, pd), lambda i: (layer, jnp.minimum(i, ntp - 1), 0)),
        pl.BlockSpec((None, tm, pd), lambda i: (layer, jnp.clip(i - ntp, 0, nts - 1), 0)),
        pl.BlockSpec((None, pd, d), lambda i: (layer, 0, 0)),
    ]
    if final_g is None:
        return pl.pallas_call(
            functools.partial(_ple_kernel, ntp=ntp),
            out_shape=jax.ShapeDtypeStruct((t, d), F32),
            grid=(t // tm,),
            in_specs=in_specs,
            out_specs=rows,
            compiler_params=_params(("parallel",), 56),
            name="gated_embedding",
        )(x, g, wg, p_prompt, p_sample, wp)
    return pl.pallas_call(
        functools.partial(_ple_final_kernel, ntp=ntp),
        out_shape=(jax.ShapeDtypeStruct((n_p, d), F32), jax.ShapeDtypeStruct((t - n_p, d), F32)),
        grid=(t // tm,),
        in_specs=in_specs + [pl.BlockSpec((1, d), lambda i: (0, 0))],
        out_specs=(pl.BlockSpec((tm, d), lambda i: (jnp.minimum(i, ntp - 1), 0)),
                   pl.BlockSpec((tm, d), lambda i: (jnp.clip(i - ntp, 0, nts - 1), 0))),
        compiler_params=_params(("arbitrary",), 56),
        name="gated_embedding_final_norm",
    )(x, g, wg, p_prompt, p_sample, wp, final_g)


def _rope_tables(pos0, seq):
    half = DK_RET // 2
    inv = ROPE_BASE ** (-jnp.arange(half, dtype=F32) / half)
    ang = (pos0 + jnp.arange(seq, dtype=jnp.int32)).astype(F32)[:, None] * inv[None, :]
    return jnp.cos(ang), jnp.sin(ang)


def _decay_tables():
    log_g = jnp.log1p(-jnp.power(2.0, -5.0 - jnp.arange(H_RET, dtype=F32)))
    idx = jnp.arange(CHUNK, dtype=F32)
    intra = jnp.exp(log_g[:, None, None] * jnp.abs(idx[:, None] - idx[None, :]))
    q_dec = jnp.exp(log_g[:, None] * (idx[None, :] + 1.0))[..., None]
    k_dec = jnp.exp(log_g[:, None] * (CHUNK - 1.0 - idx[None, :]))[..., None]
    s_dec = jnp.exp(log_g * CHUNK)[:, None, None]
    return intra, q_dec, k_dec, s_dec


def _tile(n, want):
    t = min(n, want)
    while n % t:
        t //= 2
    return t


def kernel(x_prompt, x_sample, p_prompt, p_sample, state_ret, cache_fox_k, cache_fox_v, cache_fox_logf,
           state_pool, norm_mix, w_in, b_forget, w_out, w_pool, pool_scale, norm_ffn, w_gate, w_up, w_down,
           norm_ple, ple_gate, ple_proj, final_norm):
    bp, tp, d = x_prompt.shape
    bs, ts, _ = x_sample.shape
    depth = norm_mix.shape[0]
    past = cache_fox_k.shape[2]
    n_p, n_s = bp * tp, bs * ts
    n_tok = n_p + n_s
    ret_w = H_RET * DK_RET
    fox_w = H_FOX * DH_FOX
    main_w = 4 * ret_w + 3 * fox_w
    fox_base = (4 * ret_w) // fox_w

    x = None
    p_prompt = p_prompt.reshape(depth, n_p, -1)
    p_sample = p_sample.reshape(depth, n_s, -1)

    as_rows = lambda a: a.reshape(a.shape[0], 1, a.shape[1])
    norm_mix, norm_ffn, norm_ple, pool_scale = map(as_rows, (norm_mix, norm_ffn, norm_ple, pool_scale))
    w_in_b, w_out_b, w_pool_b = w_in.astype(BF16), w_out.astype(BF16), w_pool.astype(BF16)
    ple_gate_b, ple_proj_b = ple_gate.astype(BF16), ple_proj.astype(BF16)
    w_f = jnp.pad(w_in[:, :, main_w:], ((0, 0), (0, 0), (0, LANES - H_FOX))).astype(BF16)
    b_f = jnp.pad(b_forget.astype(F32), ((0, 0), (0, LANES - H_FOX)))[:, None, :]
    cache_k = cache_fox_k.reshape(cache_fox_k.shape[:2] + (past * H_FOX, DH_FOX))
    cache_v = cache_fox_v.reshape(cache_fox_v.shape[:2] + (past * H_FOX, DH_FOX))

    decay = _decay_tables()
    rope_p = _rope_tables(0, tp)
    rope_s = _rope_tables(past, ts)
    zero_state = jnp.zeros((1, bp) + state_ret.shape[2:], F32)
    zero_buf = jnp.zeros((bp, POOL_HALO, d), F32)

    n_even = w_in.shape[0]
    kv_new = (None,) * 4
    ret_p = ret_s = None
    new_lf, new_pool_p, new_pool_s = [], [], []
    for i in range(depth):
        if i % 2 == 0:
            e = i // 2
            x_src = (x_prompt.reshape(n_p, d), x_sample.reshape(n_s, d), 0) if x is None else (x, x, n_p)
            proj, lf, *kv_new = _in_proj(x_src, norm_mix, w_in_b, w_f, b_f, kv_new, layer=i, wlayer=e,
                                         n_layers=n_even, n=main_w, n_p=n_p, n_s=n_s,
                                         k_col=4 * ret_w + fox_w, tm=_tile(math.gcd(n_p, n_s), 512))
            lf = lf[:, :H_FOX]
            lf_p = lf[:n_p].reshape(bp, tp, H_FOX)
            lf_s = lf[n_p:].reshape(bs, ts, H_FOX)
            c_row_p = _cumsum_rows(jnp.transpose(lf_p, (0, 2, 1)))
            lf_all_s = jnp.concatenate([cache_fox_logf[e].astype(F32), lf_s], axis=1)
            c_row_s = _cumsum_rows(jnp.transpose(lf_all_s, (0, 2, 1)))
            c_col_p = jnp.transpose(c_row_p, (0, 2, 1))
            c_col_s = jnp.transpose(c_row_s[:, :, past:], (0, 2, 1))

            o_r, ret_p = _retention(proj, *rope_p, decay, zero_state, None, ret_p, layer=0, out_layer=e,
                                    n_layers=n_even, batch=bp, seq=tp, row_off=0,
                                    chunks=_tile(tp, 512) // CHUNK, heads=2)
            o_r, ret_s = _retention(proj, *rope_s, decay, state_ret.astype(F32), o_r, ret_s, layer=e,
                                    out_layer=e, n_layers=n_even, batch=bs, seq=ts, row_off=n_p,
                                    chunks=ts // CHUNK, heads=H_RET)
            o_f = _fox_prompt(proj, c_col_p, c_row_p, batch=bp, seq=tp, row_off=0, tq=_tile(tp, 512),
                              col_base=fox_base)
            o_f = _fox_sample(proj, cache_k, cache_v, c_col_s, c_row_s, c_row_s[:, :, past:], o_f,
                              layer=e, batch=bs, seq=ts, row_off=n_p, tkc=_tile(past, 1024),
                              col_base=fox_base)
            x = _out_proj(o_r, o_f, w_out_b, x_src, layer=e, n_p=n_p, tm=_tile(math.gcd(n_p, n_s), 512))

            new_lf.append((lf_p, lf_s))
        else:
            o = i // 2
            buf_s = jnp.pad(state_pool[o].astype(F32), ((0, 0), (1, 0), (0, 0)))
            x_new, pool_p = _pool_mixer(x, zero_buf, norm_mix, w_pool_b, pool_scale, None, layer=i, mixer=o,
                                        batch=bp, seq=tp, row_off=0, tm=_tile(tp, 512), pos0=0)
            x, pool_s = _pool_mixer(x, buf_s, norm_mix, w_pool_b, pool_scale, x_new, layer=i, mixer=o,
                                    batch=bs, seq=ts, row_off=n_p, tm=ts, pos0=past)
            new_pool_p.append(pool_p[:, 1:])
            new_pool_s.append(pool_s[:, 1:])
        x = _ffn(x, norm_ffn, w_gate, w_up, w_down, layer=i, tm=_tile(n_tok, 1024), tf=256)
        x = _ple(x, norm_ple, ple_gate_b, p_prompt, p_sample, ple_proj_b,
                 final_norm[None, :] if i == depth - 1 else None, layer=i, n_p=n_p,
                 tm=_tile(math.gcd(n_p, n_s), 256))
    y_p, y_s = x
    k_p, k_s, v_p, v_s = kv_new
    prompt_kv = (n_even, bp, tp, H_FOX, DH_FOX)
    sample_kv = (n_even, bs, ts, H_FOX, DH_FOX)
    return (
        y_p.reshape(bp, tp, d),
        y_s.reshape(bs, ts, d),
        ret_p,
        ret_s.astype(state_ret.dtype),
        k_p.reshape(prompt_kv),
        k_s.reshape(sample_kv),
        v_p.reshape(prompt_kv),
        v_s.reshape(sample_kv),
        jnp.stack([lf[0] for lf in new_lf]),
        jnp.stack([lf[1] for lf in new_lf]),
        jnp.stack(new_pool_p),
        jnp.stack(new_pool_s),
    )
```

```python
import functools
import math

import jax
import jax.numpy as jnp
from jax import lax
from jax.experimental import pallas as pl
from jax.experimental.pallas import tpu as pltpu

F32 = jnp.float32
BF16 = jnp.bfloat16

EPS = 1e-6
NEG_INF = -1e30
ROPE_BASE = 10000.0
LOG2E = math.log2(math.e)
CHUNK = 64
H_RET = 4
DK_RET = 256
H_FOX = 8
DH_FOX = 128
POOL_WINDOWS = (2, 4, 8, 16)
POOL_HALO = 16
POOL_PAD = 8
assert POOL_WINDOWS == tuple(2 ** (n + 1) for n in range(len(POOL_WINDOWS))) and POOL_WINDOWS[-2] <= POOL_PAD
LANES = 128
CUMSUM_CHUNK = 256

_NT = (((1,), (1,)), ((), ()))
_TN = (((0,), (0,)), ((), ()))


def _params(semantics, vmem_mib):
    return pltpu.CompilerParams(dimension_semantics=semantics, vmem_limit_bytes=vmem_mib << 20)


def _rms(x, g):
    return x * lax.rsqrt(jnp.mean(x * x, axis=-1, keepdims=True) + EPS) * g


def _sigmoid(x):
    return 1.0 / (1.0 + jnp.exp(-x))


def _layer_vec(layer, d):
    return pl.BlockSpec((None, 1, d), lambda *_: (layer, 0, 0))


def _call_into(kernel, dests, n_in, **kwargs):
    dests = {k: a for k, a in dests.items() if a is not None}
    if not dests:
        return pl.pallas_call(kernel, **kwargs)
    n_d = len(dests)

    def body(*refs):
        kernel(*refs[:n_in], *refs[n_in + n_d:])

    kwargs["in_specs"] = list(kwargs["in_specs"]) + [pl.BlockSpec(memory_space=pl.ANY)] * n_d
    aliases = {n_in + pos: out for pos, out in enumerate(dests)}
    call = pl.pallas_call(body, input_output_aliases=aliases, **kwargs)
    return lambda *args: call(*args, *dests.values())


def _in_proj_kernel(xp_ref, xs_ref, g_ref, w_ref, wf_ref, bf_ref, o_ref, lf_ref, kp_ref, ks_ref, vp_ref,
                    vs_ref, h_ref, *, tm, jk, ntp):
    i = pl.program_id(0)
    j = pl.program_id(1)

    @pl.when(j == 0)
    def _():
        x = jnp.where(i < ntp, xp_ref[...], xs_ref[...])
        h = _rms(x, g_ref[...]).astype(h_ref.dtype)
        h_ref[...] = h
        z = jnp.dot(h, wf_ref[...], preferred_element_type=F32) + bf_ref[...]
        lf_ref[...] = jnp.minimum(z, 0.0) - jnp.log1p(jnp.exp(-jnp.abs(z)))

    o_ref[...] = jnp.dot(h_ref[...], w_ref[...], preferred_element_type=F32)

    def to_head_rows(dst_ref):
        for hd in range(H_FOX):
            dst_ref[pl.ds(hd, tm, stride=H_FOX), :] = o_ref[:, hd * DH_FOX:(hd + 1) * DH_FOX]

    for jj, prompt_ref, sample_ref in ((jk, kp_ref, ks_ref), (jk + 1, vp_ref, vs_ref)):
        @pl.when((j == jj) & (i < ntp))
        def _():
            to_head_rows(prompt_ref)

        @pl.when((j == jj) & (i >= ntp))
        def _():
            to_head_rows(sample_ref)


def _two_group_rows(block, ntp, nts, sample_first):
    return (pl.BlockSpec(block, lambda i, *_: (jnp.minimum(i, ntp - 1), 0)),
            pl.BlockSpec(block, lambda i, *_: (sample_first + jnp.clip(i - ntp, 0, nts - 1), 0)))


def _in_proj(x_src, g, w, wf, bf, kv_dests, *, layer, wlayer, n_layers, n, n_p, n_s, k_col, tm):
    xp, xs, xs_off = x_src
    d = xp.shape[1]
    t = n_p + n_s
    tn = H_FOX * DH_FOX
    ntp, nts = n_p // tm, n_s // tm
    rows = (tm * H_FOX, DH_FOX)
    prompt_rows = pl.BlockSpec(rows, lambda i, j: (wlayer * ntp + jnp.minimum(i, ntp - 1), 0))
    sample_rows = pl.BlockSpec(rows, lambda i, j: (wlayer * nts + jnp.clip(i - ntp, 0, nts - 1), 0))
    kv_p = jax.ShapeDtypeStruct((n_layers * n_p * H_FOX, DH_FOX), F32)
    kv_s = jax.ShapeDtypeStruct((n_layers * n_s * H_FOX, DH_FOX), F32)
    lanes = wf.shape[2]
    call = _call_into(
        functools.partial(_in_proj_kernel, tm=tm, jk=k_col // tn, ntp=ntp),
        {2 + k: a for k, a in enumerate(kv_dests)}, 6,
        out_shape=(jax.ShapeDtypeStruct((t, n), F32), jax.ShapeDtypeStruct((t, lanes), F32),
                   kv_p, kv_s, kv_p, kv_s),
        grid=(t // tm, n // tn),
        in_specs=[
            *_two_group_rows((tm, d), ntp, nts, xs_off // tm),
            _layer_vec(layer, d),
            pl.BlockSpec((None, d, tn), lambda i, j: (wlayer, 0, j)),
            pl.BlockSpec((None, d, lanes), lambda i, j: (wlayer, 0, 0)),
            pl.BlockSpec((None, 1, lanes), lambda i, j: (wlayer, 0, 0)),
        ],
        out_specs=(pl.BlockSpec((tm, tn), lambda i, j: (i, j)),
                   pl.BlockSpec((tm, lanes), lambda i, j: (i, 0)),
                   prompt_rows, sample_rows, prompt_rows, sample_rows),
        scratch_shapes=[pltpu.VMEM((tm, d), BF16)],
        compiler_params=_params(("arbitrary", "arbitrary"), 56),
        name="in_proj",
    )
    return call(xp, xs, g, w, wf, bf)


def _cumsum_kernel(lf_ref, c_ref, *, tk):
    r = lax.broadcasted_iota(jnp.int32, (CUMSUM_CHUNK, CUMSUM_CHUNK), 0)
    c = lax.broadcasted_iota(jnp.int32, (CUMSUM_CHUNK, CUMSUM_CHUNK), 1)
    upper = (r <= c).astype(F32)
    for b in range(lf_ref.shape[0]):
        carry = jnp.zeros((H_FOX, 1), F32)
        for s in range(0, tk, CUMSUM_CHUNK):
            w = min(CUMSUM_CHUNK, tk - s)
            blk = lf_ref[b, :, s:s + w]
            cs = jnp.dot(blk, upper[:w, :w], precision=lax.Precision.HIGHEST,
                         preferred_element_type=F32) + carry
            c_ref[b, :, s:s + w] = cs
            carry = cs[:, w - 1:w]


def _cumsum_rows(lf_rows):
    b, h, tk = lf_rows.shape
    bb = math.gcd(b, 8)
    return pl.pallas_call(
        functools.partial(_cumsum_kernel, tk=tk),
        out_shape=jax.ShapeDtypeStruct((b, h, tk), F32),
        grid=(b // bb,),
        in_specs=[pl.BlockSpec((bb, h, tk), lambda i: (i, 0, 0))],
        out_specs=pl.BlockSpec((bb, h, tk), lambda i: (i, 0, 0)),
        compiler_params=_params(("parallel",), 32),
        name="logf_cumsum",
    )(lf_rows)


def _rope(x, cos, sin):
    half = x.shape[-1] // 2
    x1, x2 = x[:, :half], x[:, half:]
    return jnp.concatenate([x1 * cos - x2 * sin, x1 * sin + x2 * cos], axis=-1)


def _retention_kernel(q_ref, k_ref, v_ref, g_ref, cos_ref, sin_ref, intra_ref, qdec_ref, kdec_ref,
                      sdec_ref, s0_ref, o_ref, s_out_ref, state_ref, *, chunks, heads, scale):
    i = pl.program_id(2)

    @pl.when(i == 0)
    def _():
        state_ref[...] = s0_ref[0]

    for c in range(chunks):
        rows = slice(c * CHUNK, (c + 1) * CHUNK)
        cos = cos_ref[rows, :]
        sin = sin_ref[rows, :]
        for h in range(heads):
            cols = slice(h * DK_RET, (h + 1) * DK_RET)
            q = _rope(q_ref[rows, cols], cos, sin)
            k = _rope(k_ref[rows, cols], cos, sin) * scale
            qb = q.astype(BF16)
            vb = v_ref[rows, cols].astype(BF16)
            state = state_ref[h]
            sc = lax.dot_general(qb, k.astype(BF16), _NT, preferred_element_type=F32) * intra_ref[h]
            o = (jnp.dot(sc.astype(BF16), vb, preferred_element_type=F32)
                 + jnp.dot(qb, state.astype(BF16), preferred_element_type=F32) * qdec_ref[h])
            kd = (k * kdec_ref[h]).astype(BF16)
            state_ref[h] = state * sdec_ref[h] + lax.dot_general(kd, vb, _TN, preferred_element_type=F32)
            o = o * lax.rsqrt(jnp.mean(o * o, axis=-1, keepdims=True) + EPS)
            gate = g_ref[rows, cols]
            o_ref[rows, cols] = (o * (gate * _sigmoid(gate))).astype(o_ref.dtype)

    @pl.when(i == pl.num_programs(2) - 1)
    def _():
        s_out_ref[0] = state_ref[...]


def _retention(proj, cos, sin, decay, state, dest, state_dest, *, layer, out_layer, n_layers, batch, seq,
               row_off, chunks, heads):
    tblk = chunks * CHUNK
    nblk = seq // tblk
    off = row_off // tblk
    intra, q_dec, k_dec, s_dec = decay

    def col(base):
        return pl.BlockSpec((tblk, heads * DK_RET),
                            lambda b, h, i: (off + b * nblk + i, base // heads + h))

    per_head = lambda shape: pl.BlockSpec((heads,) + shape, lambda b, h, i: (h, 0, 0))
    table = pl.BlockSpec((tblk, DK_RET // 2), lambda b, h, i: (i, 0))
    state_block = (None, 1, heads, DK_RET, DK_RET)
    call = _call_into(
        functools.partial(_retention_kernel, chunks=chunks, heads=heads, scale=DK_RET ** -0.5),
        {0: dest, 1: state_dest}, 11,
        out_shape=(jax.ShapeDtypeStruct((proj.shape[0], H_RET * DK_RET), BF16),
                   jax.ShapeDtypeStruct((n_layers, batch, H_RET, DK_RET, DK_RET), F32)),
        grid=(batch, H_RET // heads, nblk),
        in_specs=[col(0), col(H_RET), col(2 * H_RET), col(3 * H_RET), table, table,
                  per_head((CHUNK, CHUNK)), per_head((CHUNK, 1)), per_head((CHUNK, 1)),
                  per_head((1, 1)),
                  pl.BlockSpec(state_block, lambda b, h, i: (layer, b, h, 0, 0))],
        out_specs=(col(0), pl.BlockSpec(state_block, lambda b, h, i: (out_layer, b, h, 0, 0))),
        scratch_shapes=[pltpu.VMEM((heads, DK_RET, DK_RET), F32)],
        compiler_params=_params(("parallel", "parallel", "arbitrary"), 32),
        name="retention",
    )
    return call(proj, proj, proj, proj, cos, sin, intra, q_dec, k_dec, s_dec, state)


def _fox_prompt_kernel(qi_ref, kj_ref, q_ref, k_ref, v_ref, cq_ref, ck_ref, o_ref, m_ref, l_ref, acc_ref, *,
                       tq, scale):
    qi = qi_ref[pl.program_id(1)]
    kj = kj_ref[pl.program_id(1)]

    @pl.when(kj == 0)
    def _():
        m_ref[...] = jnp.full(m_ref.shape, NEG_INF, F32)
        l_ref[...] = jnp.zeros(l_ref.shape, F32)
        acc_ref[...] = jnp.zeros(acc_ref.shape, F32)

    def step(masked):
        if masked:
            key = lax.broadcasted_iota(jnp.int32, (tq, tq), 0)
            qry = lax.broadcasted_iota(jnp.int32, (tq, tq), 1)
            visible = key <= qry
        for h in range(H_FOX):
            cols = slice(h * DH_FOX, (h + 1) * DH_FOX)
            q = (q_ref[:, cols] * (scale * LOG2E)).astype(BF16)
            k = k_ref[:, cols].astype(BF16)
            cq = cq_ref[0, h:h + 1, :] * LOG2E
            s = lax.dot_general(k, q, _NT, preferred_element_type=F32) - ck_ref[0, :, h:h + 1] * LOG2E
            if masked:
                s = jnp.where(visible, s, NEG_INF)
            m_prev = m_ref[h]
            m_new = jnp.maximum(m_prev, jnp.max(s, axis=0, keepdims=True) + cq)
            p = jnp.exp2(s - (m_new - cq))
            alpha = jnp.exp2(m_prev - m_new)
            l_ref[h] = alpha * l_ref[h] + jnp.sum(p, axis=0, keepdims=True)
            pv = lax.dot_general(v_ref[:, cols].astype(BF16), p.astype(BF16), _TN,
                                 preferred_element_type=F32)
            acc_ref[h] = alpha * acc_ref[h] + pv
            m_ref[h] = m_new

    @pl.when(kj < qi)
    def _():
        step(False)

    @pl.when(kj == qi)
    def _():
        step(True)
        for h in range(H_FOX):
            o = (acc_ref[h] / l_ref[h]).T
            o_ref[:, h * DH_FOX:(h + 1) * DH_FOX] = o.astype(o_ref.dtype)


def _fox_prompt(proj, c_col, c_row, *, batch, seq, row_off, tq, col_base):
    nq = seq // tq
    off = row_off // tq
    width = H_FOX * DH_FOX
    qb, kb, vb = (col_base + n for n in range(3))

    pairs = [(i, j) for i in range(nq) for j in range(i + 1)]
    qi_tab = jnp.asarray([i for i, _ in pairs], jnp.int32)
    kj_tab = jnp.asarray([j for _, j in pairs], jnp.int32)
    grid_spec = pltpu.PrefetchScalarGridSpec(
        num_scalar_prefetch=2,
        grid=(batch, len(pairs)),
        in_specs=[
            pl.BlockSpec((tq, width), lambda b, s, qi, kj: (off + b * nq + qi[s], qb)),
            pl.BlockSpec((tq, width), lambda b, s, qi, kj: (off + b * nq + kj[s], kb)),
            pl.BlockSpec((tq, width), lambda b, s, qi, kj: (off + b * nq + kj[s], vb)),
            pl.BlockSpec((1, H_FOX, tq), lambda b, s, qi, kj: (b, 0, qi[s])),
            pl.BlockSpec((1, tq, H_FOX), lambda b, s, qi, kj: (b, kj[s], 0)),
        ],
        out_specs=pl.BlockSpec((tq, width), lambda b, s, qi, kj: (off + b * nq + qi[s], 0)),
        scratch_shapes=[pltpu.VMEM((H_FOX, 1, tq), F32), pltpu.VMEM((H_FOX, 1, tq), F32),
                        pltpu.VMEM((H_FOX, DH_FOX, tq), F32)],
    )
    return pl.pallas_call(
        functools.partial(_fox_prompt_kernel, tq=tq, scale=DH_FOX ** -0.5),
        out_shape=jax.ShapeDtypeStruct((proj.shape[0], width), BF16),
        grid_spec=grid_spec,
        compiler_params=_params(("parallel", "arbitrary"), 48),
        name="fox_prompt",
    )(qi_tab, kj_tab, proj, proj, proj, c_row, c_col)


def _fox_sample_kernel(q_ref, kc_ref, vc_ref, kn_ref, vn_ref, cq_ref, ckc_ref, ckn_ref, o_ref,
                       m_ref, l_ref, acc_ref, s_ref, p_ref, *, tq, tkc, scale):
    j = pl.program_id(1)
    last = pl.num_programs(1) - 1

    @pl.when(j == 0)
    def _():
        m_ref[...] = jnp.full(m_ref.shape, NEG_INF, F32)
        l_ref[...] = jnp.zeros(l_ref.shape, F32)
        acc_ref[...] = jnp.zeros(acc_ref.shape, F32)

    cq = jnp.concatenate([cq_ref[0, :, h:h + 1] for h in range(H_FOX)], axis=0) * LOG2E

    def scores(h, k, ck, visible=None):
        q = (q_ref[:, h * DH_FOX:(h + 1) * DH_FOX] * (scale * LOG2E)).astype(BF16)
        s = lax.dot_general(q, k, _NT, preferred_element_type=F32) - ck[h:h + 1, :]
        if visible is not None:
            s = jnp.where(visible, s, NEG_INF)
        s_ref[h * tq:(h + 1) * tq, :s.shape[1]] = s

    def softmax_update(width):
        s = s_ref[:, :width]
        m_prev = m_ref[...]
        m_new = jnp.maximum(m_prev, jnp.max(s, axis=-1, keepdims=True) + cq)
        p = jnp.exp2(s - (m_new - cq))
        alpha = jnp.exp2(m_prev - m_new)
        l_ref[...] = alpha * l_ref[...] + jnp.sum(p, axis=-1, keepdims=True)
        m_ref[...] = m_new
        p_ref[:, :width] = p.astype(BF16)
        return alpha

    def values(h, v, alpha, width):
        rows = slice(h * tq, (h + 1) * tq)
        acc_ref[h] = alpha[rows] * acc_ref[h] + jnp.dot(p_ref[rows, :width], v, preferred_element_type=F32)

    ck = ckc_ref[0] * LOG2E
    for h in range(H_FOX):
        scores(h, kc_ref[pl.ds(h, tkc, stride=H_FOX), :].astype(BF16), ck)
    alpha = softmax_update(tkc)
    for h in range(H_FOX):
        values(h, vc_ref[pl.ds(h, tkc, stride=H_FOX), :].astype(BF16), alpha, tkc)

    @pl.when(j == last)
    def _():
        row = lax.broadcasted_iota(jnp.int32, (tq, tq), 0)
        col = lax.broadcasted_iota(jnp.int32, (tq, tq), 1)
        visible = col <= row
        ck = ckn_ref[0] * LOG2E
        for h in range(H_FOX):
            scores(h, kn_ref[:, h * DH_FOX:(h + 1) * DH_FOX].astype(BF16), ck, visible)
        alpha = softmax_update(tq)
        for h in range(H_FOX):
            cols = slice(h * DH_FOX, (h + 1) * DH_FOX)
            values(h, vn_ref[:, cols].astype(BF16), alpha, tq)
            o_ref[:, cols] = (acc_ref[h] / l_ref[h * tq:(h + 1) * tq]).astype(o_ref.dtype)


def _fox_sample(proj, cache_k, cache_v, c_col, c_row_cache, c_row_new, dest, *, layer, batch, seq,
                row_off, tkc, col_base):
    past = cache_k.shape[2] // H_FOX
    nkc = past // tkc
    off = row_off // seq
    width = H_FOX * DH_FOX
    qb, kb, vb = (col_base + n for n in range(3))
    cache_spec = pl.BlockSpec((None, None, tkc * H_FOX, DH_FOX), lambda b, j: (layer, b, j, 0))

    call = _call_into(
        functools.partial(_fox_sample_kernel, tq=seq, tkc=tkc, scale=DH_FOX ** -0.5), {0: dest}, 8,
        out_shape=jax.ShapeDtypeStruct((proj.shape[0], width), BF16),
        grid=(batch, nkc),
        in_specs=[
            pl.BlockSpec((seq, width), lambda b, j: (off + b, qb)),
            cache_spec,
            cache_spec,
            pl.BlockSpec((seq, width), lambda b, j: (off + b, kb)),
            pl.BlockSpec((seq, width), lambda b, j: (off + b, vb)),
            pl.BlockSpec((1, seq, H_FOX), lambda b, j: (b, 0, 0)),
            pl.BlockSpec((1, H_FOX, tkc), lambda b, j: (b, 0, j)),
            pl.BlockSpec((1, H_FOX, seq), lambda b, j: (b, 0, 0)),
        ],
        out_specs=pl.BlockSpec((seq, width), lambda b, j: (off + b, 0)),
        scratch_shapes=[pltpu.VMEM((H_FOX * seq, 1), F32), pltpu.VMEM((H_FOX * seq, 1), F32),
                        pltpu.VMEM((H_FOX, seq, DH_FOX), F32),
                        pltpu.VMEM((H_FOX * seq, tkc), F32), pltpu.VMEM((H_FOX * seq, tkc), BF16)],
        compiler_params=_params(("parallel", "arbitrary"), 48),
        name="fox_sample",
    )
    return call(proj, cache_k, cache_v, proj, proj, c_col, c_row_cache, c_row_new)


def _out_proj_kernel(a_ref, b_ref, w_ref, xp_ref, xs_ref, o_ref, *, ntp):
    ka = a_ref.shape[1]
    o_ref[...] = (jnp.where(pl.program_id(0) < ntp, xp_ref[...], xs_ref[...])
                  + jnp.dot(a_ref[...], w_ref[:ka, :], preferred_element_type=F32)
                  + jnp.dot(b_ref[...], w_ref[ka:, :], preferred_element_type=F32))


def _out_proj(a, b, w, x_src, *, layer, n_p, tm):
    xp, xs, xs_off = x_src
    t, d = a.shape[0], xp.shape[1]
    ka, kb = a.shape[1], b.shape[1]
    ntp = n_p // tm
    return pl.pallas_call(
        functools.partial(_out_proj_kernel, ntp=ntp),
        out_shape=jax.ShapeDtypeStruct((t, d), F32),
        grid=(t // tm,),
        in_specs=[
            pl.BlockSpec((tm, ka), lambda i: (i, 0)),
            pl.BlockSpec((tm, kb), lambda i: (i, 0)),
            pl.BlockSpec((None, ka + kb, d), lambda i: (layer, 0, 0)),
            *_two_group_rows((tm, d), ntp, (t - n_p) // tm, xs_off // tm),
        ],
        out_specs=pl.BlockSpec((tm, d), lambda i: (i, 0)),
        compiler_params=_params(("parallel",), 56),
        name="out_proj",
    )(a, b, w, xp, xs)


def _pool_kernel(x_ref, halo_ref, buf_ref, g_ref, w_ref, ps_ref, o_ref, bo_ref, full_ref, sum_a_ref,
                 sum_b_ref, *, tm, pos0):
    i = pl.program_id(1)
    g = g_ref[...]
    x = x_ref[...]
    u = _rms(x, g)
    hist = jnp.where(i == 0, buf_ref[0], _rms(halo_ref[...], g))
    lo, top = POOL_PAD, POOL_PAD + POOL_HALO
    end = top + tm
    for ref in (full_ref, sum_a_ref, sum_b_ref):
        ref[0:lo, :] = jnp.zeros((lo, x.shape[1]), F32)
    full_ref[lo:top, :] = hist
    full_ref[top:end, :] = u
    pos = pos0 + i * tm + lax.broadcasted_iota(jnp.int32, (tm, 1), 0)
    gc = x.shape[1] // len(POOL_WINDOWS)
    src, width = full_ref, 1
    for n, win in enumerate(POOL_WINDOWS):
        cols = slice(n * gc, (n + 1) * gc)
        if n == len(POOL_WINDOWS) - 1:
            total = src[top:end, cols] + src[top - width:end - width, cols]
        else:
            dst = (sum_a_ref, sum_b_ref)[n % 2]
            dst[lo:end, n * gc:] = src[lo:end, n * gc:] + src[lo - width:end - width, n * gc:]
            total = dst[top:end, cols]
            src = dst
        width = win
        inv_cnt = 1.0 / jnp.minimum(win, pos + 1).astype(F32)
        diff = (total * inv_cnt - u[:, cols]).astype(BF16)
        y = jnp.dot(diff, w_ref[n], preferred_element_type=F32) * ps_ref[:, cols]
        o_ref[:, cols] = x[:, cols] + y
    bo_ref[0] = u[tm - POOL_HALO:, :]


def _pool_mixer(x, buf, g, w, ps, dest, *, layer, mixer, batch, seq, row_off, tm, pos0):
    d = x.shape[1]
    nt = seq // tm
    off = row_off // tm
    per_halo = tm // POOL_HALO
    halo_off = row_off // POOL_HALO

    def halo_map(b, i):
        return (jnp.maximum(halo_off + (b * nt + i) * per_halo - 1, 0), 0)

    rows = pl.BlockSpec((tm, d), lambda b, i: (off + b * nt + i, 0))
    call = _call_into(
        functools.partial(_pool_kernel, tm=tm, pos0=pos0), {0: dest}, 6,
        out_shape=(jax.ShapeDtypeStruct(x.shape, F32),
                   jax.ShapeDtypeStruct((batch, POOL_HALO, d), F32)),
        grid=(batch, nt),
        in_specs=[
            rows,
            pl.BlockSpec((POOL_HALO, d), halo_map),
            pl.BlockSpec((1, POOL_HALO, d), lambda b, i: (b, 0, 0)),
            _layer_vec(layer, d),
            pl.BlockSpec((None,) + w.shape[1:], lambda b, i: (mixer, 0, 0, 0)),
            _layer_vec(mixer, d),
        ],
        out_specs=(rows, pl.BlockSpec((1, POOL_HALO, d), lambda b, i: (b, 0, 0))),
        scratch_shapes=[pltpu.VMEM((POOL_PAD + POOL_HALO + tm, d), F32)] * 3,
        compiler_params=_params(("parallel", "arbitrary"), 48),
        name="pool_mixer",
    )
    return call(x, x, buf, g, w, ps)


def _ffn_kernel(x_ref, g_ref, wg_ref, wu_ref, wd_ref, o_ref, h_ref):
    @pl.when(pl.program_id(1) == 0)
    def _():
        x = x_ref[...]
        h_ref[...] = _rms(x, g_ref[...]).astype(h_ref.dtype)
        o_ref[...] = x

    h = h_ref[...]
    a = jnp.dot(h, wg_ref[...].astype(BF16), preferred_element_type=F32)
    u = jnp.dot(h, wu_ref[...].astype(BF16), preferred_element_type=F32)
    act = (a * _sigmoid(a) * u).astype(BF16)
    o_ref[...] += jnp.dot(act, wd_ref[...].astype(BF16), preferred_element_type=F32)


def _ffn(x, g, wg, wu, wd, *, layer, tm, tf):
    t, d = x.shape
    f = wg.shape[2]
    return pl.pallas_call(
        _ffn_kernel,
        out_shape=jax.ShapeDtypeStruct((t, d), F32),
        grid=(t // tm, f // tf),
        in_specs=[
            pl.BlockSpec((tm, d), lambda i, j: (i, 0)),
            _layer_vec(layer, d),
            pl.BlockSpec((None, d, tf), lambda i, j: (layer, 0, j)),
            pl.BlockSpec((None, d, tf), lambda i, j: (layer, 0, j)),
            pl.BlockSpec((None, tf, d), lambda i, j: (layer, j, 0)),
        ],
        out_specs=pl.BlockSpec((tm, d), lambda i, j: (i, 0)),
        scratch_shapes=[pltpu.VMEM((tm, d), BF16)],
        compiler_params=_params(("parallel", "arbitrary"), 56),
        name="swiglu_ffn",
    )(x, g, wg, wu, wd)


def _ple_update(x_ref, g_ref, wg_ref, pp_ref, ps_ref, wp_ref, ntp):
    x = x_ref[...]
    h = _rms(x, g_ref[...]).astype(BF16)
    gate = _sigmoid(jnp.dot(h, wg_ref[...], preferred_element_type=F32))
    p = jnp.where(pl.program_id(0) < ntp, pp_ref[...], ps_ref[...])
    emb = jnp.dot(p.astype(BF16), wp_ref[...], preferred_element_type=F32)
    return x + gate * emb


def _ple_kernel(x_ref, g_ref, wg_ref, pp_ref, ps_ref, wp_ref, o_ref, *, ntp):
    o_ref[...] = _ple_update(x_ref, g_ref, wg_ref, pp_ref, ps_ref, wp_ref, ntp)


def _ple_final_kernel(x_ref, g_ref, wg_ref, pp_ref, ps_ref, wp_ref, gf_ref, yp_ref, ys_ref, *, ntp):
    y = _rms(_ple_update(x_ref, g_ref, wg_ref, pp_ref, ps_ref, wp_ref, ntp), gf_ref[...])

    @pl.when(pl.program_id(0) < ntp)
    def _():
        yp_ref[...] = y

    @pl.when(pl.program_id(0) >= ntp)
    def _():
        ys_ref[...] = y


def _ple(x, g, wg, p_prompt, p_sample, wp, final_g, *, layer, n_p, tm):
    t, d = x.shape
    pd = p_prompt.shape[2]
    ntp, nts = n_p // tm, (t - n_p) // tm
    rows = pl.BlockSpec((tm, d), lambda i: (i, 0))
    in_specs = [
        rows,
        _layer_vec(layer, d),
        pl.BlockSpec((None, d, d), lambda i: (layer, 0, 0)),
        pl.BlockSpec((None, tm, pd), lambda i: (layer, jnp.minimum(i, ntp - 1), 0)),
        pl.BlockSpec((None, tm, pd), lambda i: (layer, jnp.clip(i - ntp, 0, nts - 1), 0)),
        pl.BlockSpec((None, pd, d), lambda i: (layer, 0, 0)),
    ]
    if final_g is None:
        return pl.pallas_call(
            functools.partial(_ple_kernel, ntp=ntp),
            out_shape=jax.ShapeDtypeStruct((t, d), F32),
            grid=(t // tm,),
            in_specs=in_specs,
            out_specs=rows,
            compiler_params=_params(("parallel",), 56),
            name="gated_embedding",
        )(x, g, wg, p_prompt, p_sample, wp)
    return pl.pallas_call(
        functools.partial(_ple_final_kernel, ntp=ntp),
        out_shape=(jax.ShapeDtypeStruct((n_p, d), F32), jax.ShapeDtypeStruct((t - n_p, d), F32)),
        grid=(t // tm,),
        in_specs=in_specs + [pl.BlockSpec((1, d), lambda i: (0, 0))],
        out_specs=(pl.BlockSpec((tm, d), lambda i: (jnp.minimum(i, ntp - 1), 0)),
                   pl.BlockSpec((tm, d), lambda i: (jnp.clip(i - ntp, 0, nts - 1), 0))),
        compiler_params=_params(("arbitrary",), 56),
        name="gated_embedding_final_norm",
    )(x, g, wg, p_prompt, p_sample, wp, final_g)


def _rope_tables(pos0, seq):
    half = DK_RET // 2
    inv = ROPE_BASE ** (-jnp.arange(half, dtype=F32) / half)
    ang = (pos0 + jnp.arange(seq, dtype=jnp.int32)).astype(F32)[:, None] * inv[None, :]
    return jnp.cos(ang), jnp.sin(ang)


def _decay_tables():
    log_g = jnp.log1p(-jnp.power(2.0, -5.0 - jnp.arange(H_RET, dtype=F32)))
    idx = jnp.arange(CHUNK, dtype=F32)
    intra = jnp.exp(log_g[:, None, None] * jnp.abs(idx[:, None] - idx[None, :]))
    q_dec = jnp.exp(log_g[:, None] * (idx[None, :] + 1.0))[..., None]
    k_dec = jnp.exp(log_g[:, None] * (CHUNK - 1.0 - idx[None, :]))[..., None]
    s_dec = jnp.exp(log_g * CHUNK)[:, None, None]
    return intra, q_dec, k_dec, s_dec


def _tile(n, want):
    t = min(n, want)
    while n % t:
        t //= 2
    return t


def kernel(x_prompt, x_sample, p_prompt, p_sample, state_ret, cache_fox_k, cache_fox_v, cache_fox_logf,
           state_pool, norm_mix, w_in, b_forget, w_out, w_pool, pool_scale, norm_ffn, w_gate, w_up, w_down,
           norm_ple, ple_gate, ple_proj, final_norm):
    bp, tp, d = x_prompt.shape
    bs, ts, _ = x_sample.shape
    depth = norm_mix.shape[0]
    past = cache_fox_k.shape[2]
    n_p, n_s = bp * tp, bs * ts
    n_tok = n_p + n_s
    ret_w = H_RET * DK_RET
    fox_w = H_FOX * DH_FOX
    main_w = 4 * ret_w + 3 * fox_w
    fox_base = (4 * ret_w) // fox_w

    x = None
    p_prompt = p_prompt.reshape(depth, n_p, -1)
    p_sample = p_sample.reshape(depth, n_s, -1)

    as_rows = lambda a: a.reshape(a.shape[0], 1, a.shape[1])
    norm_mix, norm_ffn, norm_ple, pool_scale = map(as_rows, (norm_mix, norm_ffn, norm_ple, pool_scale))
    w_in_b, w_out_b, w_pool_b = w_in.astype(BF16), w_out.astype(BF16), w_pool.astype(BF16)
    ple_gate_b, ple_proj_b = ple_gate.astype(BF16), ple_proj.astype(BF16)
    w_f = jnp.pad(w_in[:, :, main_w:], ((0, 0), (0, 0), (0, LANES - H_FOX))).astype(BF16)
    b_f = jnp.pad(b_forget.astype(F32), ((0, 0), (0, LANES - H_FOX)))[:, None, :]
    cache_k = cache_fox_k.reshape(cache_fox_k.shape[:2] + (past * H_FOX, DH_FOX))
    cache_v = cache_fox_v.reshape(cache_fox_v.shape[:2] + (past * H_FOX, DH_FOX))

    decay = _decay_tables()
    rope_p = _rope_tables(0, tp)
    rope_s = _rope_tables(past, ts)
    zero_state = jnp.zeros((1, bp) + state_ret.shape[2:], F32)
    zero_buf = jnp.zeros((bp, POOL_HALO, d), F32)

    n_even = w_in.shape[0]
    kv_new = (None,) * 4
    ret_p = ret_s = None
    new_lf, new_pool_p, new_pool_s = [], [], []
    for i in range(depth):
        if i % 2 == 0:
            e = i // 2
            x_src = (x_prompt.reshape(n_p, d), x_sample.reshape(n_s, d), 0) if x is None else (x, x, n_p)
            proj, lf, *kv_new = _in_proj(x_src, norm_mix, w_in_b, w_f, b_f, kv_new, layer=i, wlayer=e,
                                         n_layers=n_even, n=main_w, n_p=n_p, n_s=n_s,
                                         k_col=4 * ret_w + fox_w, tm=_tile(math.gcd(n_p, n_s), 512))
            lf = lf[:, :H_FOX]
            lf_p = lf[:n_p].reshape(bp, tp, H_FOX)
            lf_s = lf[n_p:].reshape(bs, ts, H_FOX)
            c_row_p = _cumsum_rows(jnp.transpose(lf_p, (0, 2, 1)))
            lf_all_s = jnp.concatenate([cache_fox_logf[e].astype(F32), lf_s], axis=1)
            c_row_s = _cumsum_rows(jnp.transpose(lf_all_s, (0, 2, 1)))
            c_col_p = jnp.transpose(c_row_p, (0, 2, 1))
            c_col_s = jnp.transpose(c_row_s[:, :, past:], (0, 2, 1))

            o_r, ret_p = _retention(proj, *rope_p, decay, zero_state, None, ret_p, layer=0, out_layer=e,
                                    n_layers=n_even, batch=bp, seq=tp, row_off=0,
                                    chunks=_tile(tp, 512) // CHUNK, heads=2)
            o_r, ret_s = _retention(proj, *rope_s, decay, state_ret.astype(F32), o_r, ret_s, layer=e,
                                    out_layer=e, n_layers=n_even, batch=bs, seq=ts, row_off=n_p,
                                    chunks=ts // CHUNK, heads=H_RET)
            o_f = _fox_prompt(proj, c_col_p, c_row_p, batch=bp, seq=tp, row_off=0, tq=_tile(tp, 512),
                              col_base=fox_base)
            o_f = _fox_sample(proj, cache_k, cache_v, c_col_s, c_row_s, c_row_s[:, :, past:], o_f,
                              layer=e, batch=bs, seq=ts, row_off=n_p, tkc=_tile(past, 2048),
                              col_base=fox_base)
            x = _out_proj(o_r, o_f, w_out_b, x_src, layer=e, n_p=n_p, tm=_tile(math.gcd(n_p, n_s), 512))

            new_lf.append((lf_p, lf_s))
        else:
            o = i // 2
            buf_s = jnp.pad(state_pool[o].astype(F32), ((0, 0), (1, 0), (0, 0)))
            x_new, pool_p = _pool_mixer(x, zero_buf, norm_mix, w_pool_b, pool_scale, None, layer=i, mixer=o,
                                        batch=bp, seq=tp, row_off=0, tm=_tile(tp, 512), pos0=0)
            x, pool_s = _pool_mixer(x, buf_s, norm_mix, w_pool_b, pool_scale, x_new, layer=i, mixer=o,
                                    batch=bs, seq=ts, row_off=n_p, tm=ts, pos0=past)
            new_pool_p.append(pool_p[:, 1:])
            new_pool_s.append(pool_s[:, 1:])
        x = _ffn(x, norm_ffn, w_gate, w_up, w_down, layer=i, tm=_tile(n_tok, 1024), tf=256)
        x = _ple(x, norm_ple, ple_gate_b, p_prompt, p_sample, ple_proj_b,
                 final_norm[None, :] if i == depth - 1 else None, layer=i, n_p=n_p,
                 tm=_tile(math.gcd(n_p, n_s), 512))
    y_p, y_s = x
    k_p, k_s, v_p, v_s = kv_new
    prompt_kv = (n_even, bp, tp, H_FOX, DH_FOX)
    sample_kv = (n_even, bs, ts, H_FOX, DH_FOX)
    return (
        y_p.reshape(bp, tp, d),
        y_s.reshape(bs, ts, d),
        ret_p,
        ret_s.astype(state_ret.dtype),
        k_p.reshape(prompt_kv),
        k_s.reshape(sample_kv),
        v_p.reshape(prompt_kv),
        v_s.reshape(sample_kv),
        jnp.stack([lf[0] for lf in new_lf]),
        jnp.stack([lf[1] for lf in new_lf]),
        jnp.stack(new_pool_p),
        jnp.stack(new_pool_s),
    )
```

```python
import functools
import math

import jax
import jax.numpy as jnp
from jax import lax
from jax.experimental import pallas as pl
from jax.experimental.pallas import tpu as pltpu

F32 = jnp.float32
BF16 = jnp.bfloat16

EPS = 1e-6
NEG_INF = -1e30
ROPE_BASE = 10000.0
LOG2E = math.log2(math.e)
CHUNK = 64
H_RET = 4
DK_RET = 256
H_FOX = 8
DH_FOX = 128
POOL_WINDOWS = (2, 4, 8, 16)
POOL_HALO = 16
POOL_PAD = 8
assert POOL_WINDOWS == tuple(2 ** (n + 1) for n in range(len(POOL_WINDOWS))) and POOL_WINDOWS[-2] <= POOL_PAD
LANES = 128
CUMSUM_CHUNK = 256

_NT = (((1,), (1,)), ((), ()))
_TN = (((0,), (0,)), ((), ()))


def _params(semantics, vmem_mib):
    return pltpu.CompilerParams(dimension_semantics=semantics, vmem_limit_bytes=vmem_mib << 20)


def _rms(x, g):
    return x * lax.rsqrt(jnp.mean(x * x, axis=-1, keepdims=True) + EPS) * g


def _sigmoid(x):
    return 1.0 / (1.0 + jnp.exp(-x))


def _layer_vec(layer, d):
    return pl.BlockSpec((None, 1, d), lambda *_: (layer, 0, 0))


def _call_into(kernel, dests, n_in, **kwargs):
    dests = {k: a for k, a in dests.items() if a is not None}
    if not dests:
        return pl.pallas_call(kernel, **kwargs)
    n_d = len(dests)

    def body(*refs):
        kernel(*refs[:n_in], *refs[n_in + n_d:])

    kwargs["in_specs"] = list(kwargs["in_specs"]) + [pl.BlockSpec(memory_space=pl.ANY)] * n_d
    aliases = {n_in + pos: out for pos, out in enumerate(dests)}
    call = pl.pallas_call(body, input_output_aliases=aliases, **kwargs)
    return lambda *args: call(*args, *dests.values())


W_RING = 3


def _in_proj_kernel(xp_ref, xs_ref, g_ref, w_hbm, wf_ref, bf_ref, o_ref, lf_ref, kp_ref, ks_ref, vp_ref,
                    vs_ref, h_ref, w_ring, w_sem, *, tm, tn, jk, ntp, wlayer, ni, nj):
    i = pl.program_id(0)
    j = pl.program_id(1)
    step = i * nj + j
    n_steps = ni * nj

    def tile_copy(col_tile, slot):
        return pltpu.make_async_copy(w_hbm.at[wlayer, :, pl.ds(pl.multiple_of(col_tile * tn, tn), tn)],
                                     w_ring.at[slot], w_sem.at[slot])

    @pl.when(step == 0)
    def _():
        tile_copy(0, 0).start()
        tile_copy(1 % nj, 1).start()

    ahead = step + (W_RING - 1)

    @pl.when(ahead < n_steps)
    def _():
        tile_copy(lax.rem(j + (W_RING - 1), nj), lax.rem(ahead, W_RING)).start()

    slot = lax.rem(step, W_RING)

    @pl.when(j == 0)
    def _():
        x = jnp.where(i < ntp, xp_ref[...], xs_ref[...])
        h = _rms(x, g_ref[...]).astype(h_ref.dtype)
        h_ref[...] = h
        z = jnp.dot(h, wf_ref[...], preferred_element_type=F32) + bf_ref[...]
        lf_ref[...] = jnp.minimum(z, 0.0) - jnp.log1p(jnp.exp(-jnp.abs(z)))

    tile_copy(j, slot).wait()
    o_ref[...] = jnp.dot(h_ref[...], w_ring[slot], preferred_element_type=F32)

    def to_head_rows(dst_ref):
        for hd in range(H_FOX):
            dst_ref[pl.ds(hd, tm, stride=H_FOX), :] = o_ref[:, hd * DH_FOX:(hd + 1) * DH_FOX]

    for jj, prompt_ref, sample_ref in ((jk, kp_ref, ks_ref), (jk + 1, vp_ref, vs_ref)):
        @pl.when((j == jj) & (i < ntp))
        def _():
            to_head_rows(prompt_ref)

        @pl.when((j == jj) & (i >= ntp))
        def _():
            to_head_rows(sample_ref)


def _two_group_rows(block, ntp, nts, sample_first):
    return (pl.BlockSpec(block, lambda i, *_: (jnp.minimum(i, ntp - 1), 0)),
            pl.BlockSpec(block, lambda i, *_: (sample_first + jnp.clip(i - ntp, 0, nts - 1), 0)))


def _in_proj(x_src, g, w, wf, bf, kv_dests, *, layer, wlayer, n_layers, n, n_p, n_s, k_col, tm):
    xp, xs, xs_off = x_src
    d = xp.shape[1]
    t = n_p + n_s
    tn = H_FOX * DH_FOX
    ntp, nts = n_p // tm, n_s // tm
    rows = (tm * H_FOX, DH_FOX)
    prompt_rows = pl.BlockSpec(rows, lambda i, j: (wlayer * ntp + jnp.minimum(i, ntp - 1), 0))
    sample_rows = pl.BlockSpec(rows, lambda i, j: (wlayer * nts + jnp.clip(i - ntp, 0, nts - 1), 0))
    kv_p = jax.ShapeDtypeStruct((n_layers * n_p * H_FOX, DH_FOX), F32)
    kv_s = jax.ShapeDtypeStruct((n_layers * n_s * H_FOX, DH_FOX), F32)
    lanes = wf.shape[2]
    call = _call_into(
        functools.partial(_in_proj_kernel, tm=tm, tn=tn, jk=k_col // tn, ntp=ntp, wlayer=wlayer,
                          ni=t // tm, nj=n // tn),
        {2 + k: a for k, a in enumerate(kv_dests)}, 6,
        out_shape=(jax.ShapeDtypeStruct((t, n), F32), jax.ShapeDtypeStruct((t, lanes), F32),
                   kv_p, kv_s, kv_p, kv_s),
        grid=(t // tm, n // tn),
        in_specs=[
            *_two_group_rows((tm, d), ntp, nts, xs_off // tm),
            _layer_vec(layer, d),
            pl.BlockSpec(memory_space=pl.ANY),
            pl.BlockSpec((None, d, lanes), lambda i, j: (wlayer, 0, 0)),
            pl.BlockSpec((None, 1, lanes), lambda i, j: (wlayer, 0, 0)),
        ],
        out_specs=(pl.BlockSpec((tm, tn), lambda i, j: (i, j)),
                   pl.BlockSpec((tm, lanes), lambda i, j: (i, 0)),
                   prompt_rows, sample_rows, prompt_rows, sample_rows),
        scratch_shapes=[pltpu.VMEM((tm, d), BF16), pltpu.VMEM((W_RING, d, tn), BF16),
                        pltpu.SemaphoreType.DMA((W_RING,))],
        compiler_params=_params(("arbitrary", "arbitrary"), 58),
        name="in_proj",
    )
    return call(xp, xs, g, w, wf, bf)


def _cumsum_kernel(lf_ref, c_ref, *, tk):
    r = lax.broadcasted_iota(jnp.int32, (CUMSUM_CHUNK, CUMSUM_CHUNK), 0)
    c = lax.broadcasted_iota(jnp.int32, (CUMSUM_CHUNK, CUMSUM_CHUNK), 1)
    upper = (r <= c).astype(F32)
    for b in range(lf_ref.shape[0]):
        carry = jnp.zeros((H_FOX, 1), F32)
        for s in range(0, tk, CUMSUM_CHUNK):
            w = min(CUMSUM_CHUNK, tk - s)
            blk = lf_ref[b, :, s:s + w]
            cs = jnp.dot(blk, upper[:w, :w], precision=lax.Precision.HIGHEST,
                         preferred_element_type=F32) + carry
            c_ref[b, :, s:s + w] = cs
            carry = cs[:, w - 1:w]


def _cumsum_rows(lf_rows):
    b, h, tk = lf_rows.shape
    bb = math.gcd(b, 8)
    return pl.pallas_call(
        functools.partial(_cumsum_kernel, tk=tk),
        out_shape=jax.ShapeDtypeStruct((b, h, tk), F32),
        grid=(b // bb,),
        in_specs=[pl.BlockSpec((bb, h, tk), lambda i: (i, 0, 0))],
        out_specs=pl.BlockSpec((bb, h, tk), lambda i: (i, 0, 0)),
        compiler_params=_params(("parallel",), 32),
        name="logf_cumsum",
    )(lf_rows)


def _rope(x, cos, sin):
    half = x.shape[-1] // 2
    x1, x2 = x[:, :half], x[:, half:]
    return jnp.concatenate([x1 * cos - x2 * sin, x1 * sin + x2 * cos], axis=-1)


def _retention_kernel(q_ref, k_ref, v_ref, g_ref, cos_ref, sin_ref, intra_ref, qdec_ref, kdec_ref,
                      sdec_ref, s0_ref, o_ref, s_out_ref, state_ref, *, chunks, heads, scale):
    i = pl.program_id(2)

    @pl.when(i == 0)
    def _():
        state_ref[...] = s0_ref[0]

    for c in range(chunks):
        rows = slice(c * CHUNK, (c + 1) * CHUNK)
        cos = cos_ref[rows, :]
        sin = sin_ref[rows, :]
        for h in range(heads):
            cols = slice(h * DK_RET, (h + 1) * DK_RET)
            q = _rope(q_ref[rows, cols], cos, sin)
            k = _rope(k_ref[rows, cols], cos, sin) * scale
            qb = q.astype(BF16)
            vb = v_ref[rows, cols].astype(BF16)
            state = state_ref[h]
            sc = lax.dot_general(qb, k.astype(BF16), _NT, preferred_element_type=F32) * intra_ref[h]
            o = (jnp.dot(sc.astype(BF16), vb, preferred_element_type=F32)
                 + jnp.dot(qb, state.astype(BF16), preferred_element_type=F32) * qdec_ref[h])
            kd = (k * kdec_ref[h]).astype(BF16)
            state_ref[h] = state * sdec_ref[h] + lax.dot_general(kd, vb, _TN, preferred_element_type=F32)
            o = o * lax.rsqrt(jnp.mean(o * o, axis=-1, keepdims=True) + EPS)
            gate = g_ref[rows, cols]
            o_ref[rows, cols] = (o * (gate * _sigmoid(gate))).astype(o_ref.dtype)

    @pl.when(i == pl.num_programs(2) - 1)
    def _():
        s_out_ref[0] = state_ref[...]


def _retention(proj, cos, sin, decay, state, dest, state_dest, *, layer, out_layer, n_layers, batch, seq,
               row_off, chunks, heads):
    tblk = chunks * CHUNK
    nblk = seq // tblk
    off = row_off // tblk
    intra, q_dec, k_dec, s_dec = decay

    def col(base):
        return pl.BlockSpec((tblk, heads * DK_RET),
                            lambda b, h, i: (off + b * nblk + i, base // heads + h))

    per_head = lambda shape: pl.BlockSpec((heads,) + shape, lambda b, h, i: (h, 0, 0))
    table = pl.BlockSpec((tblk, DK_RET // 2), lambda b, h, i: (i, 0))
    state_block = (None, 1, heads, DK_RET, DK_RET)
    call = _call_into(
        functools.partial(_retention_kernel, chunks=chunks, heads=heads, scale=DK_RET ** -0.5),
        {0: dest, 1: state_dest}, 11,
        out_shape=(jax.ShapeDtypeStruct((proj.shape[0], H_RET * DK_RET), BF16),
                   jax.ShapeDtypeStruct((n_layers, batch, H_RET, DK_RET, DK_RET), F32)),
        grid=(batch, H_RET // heads, nblk),
        in_specs=[col(0), col(H_RET), col(2 * H_RET), col(3 * H_RET), table, table,
                  per_head((CHUNK, CHUNK)), per_head((CHUNK, 1)), per_head((CHUNK, 1)),
                  per_head((1, 1)),
                  pl.BlockSpec(state_block, lambda b, h, i: (layer, b, h, 0, 0))],
        out_specs=(col(0), pl.BlockSpec(state_block, lambda b, h, i: (out_layer, b, h, 0, 0))),
        scratch_shapes=[pltpu.VMEM((heads, DK_RET, DK_RET), F32)],
        compiler_params=_params(("parallel", "parallel", "arbitrary"), 32),
        name="retention",
    )
    return call(proj, proj, proj, proj, cos, sin, intra, q_dec, k_dec, s_dec, state)


def _fox_prompt_kernel(qi_ref, kj_ref, q_ref, k_ref, v_ref, cq_ref, ck_ref, o_ref, m_ref, l_ref, acc_ref, *,
                       tq, scale):
    qi = qi_ref[pl.program_id(1)]
    kj = kj_ref[pl.program_id(1)]

    @pl.when(kj == 0)
    def _():
        m_ref[...] = jnp.full(m_ref.shape, NEG_INF, F32)
        l_ref[...] = jnp.zeros(l_ref.shape, F32)
        acc_ref[...] = jnp.zeros(acc_ref.shape, F32)

    def step(masked):
        if masked:
            key = lax.broadcasted_iota(jnp.int32, (tq, tq), 0)
            qry = lax.broadcasted_iota(jnp.int32, (tq, tq), 1)
            visible = key <= qry
        for h in range(H_FOX):
            cols = slice(h * DH_FOX, (h + 1) * DH_FOX)
            q = (q_ref[:, cols] * (scale * LOG2E)).astype(BF16)
            k = k_ref[:, cols].astype(BF16)
            cq = cq_ref[0, h:h + 1, :] * LOG2E
            s = lax.dot_general(k, q, _NT, preferred_element_type=F32) - ck_ref[0, :, h:h + 1] * LOG2E
            if masked:
                s = jnp.where(visible, s, NEG_INF)
            m_prev = m_ref[h]
            m_new = jnp.maximum(m_prev, jnp.max(s, axis=0, keepdims=True) + cq)
            p = jnp.exp2(s - (m_new - cq))
            alpha = jnp.exp2(m_prev - m_new)
            l_ref[h] = alpha * l_ref[h] + jnp.sum(p, axis=0, keepdims=True)
            pv = lax.dot_general(v_ref[:, cols].astype(BF16), p.astype(BF16), _TN,
                                 preferred_element_type=F32)
            acc_ref[h] = alpha * acc_ref[h] + pv
            m_ref[h] = m_new

    @pl.when(kj < qi)
    def _():
        step(False)

    @pl.when(kj == qi)
    def _():
        step(True)
        for h in range(H_FOX):
            o = (acc_ref[h] / l_ref[h]).T
            o_ref[:, h * DH_FOX:(h + 1) * DH_FOX] = o.astype(o_ref.dtype)


def _fox_prompt(proj, c_col, c_row, *, batch, seq, row_off, tq, col_base):
    nq = seq // tq
    off = row_off // tq
    width = H_FOX * DH_FOX
    qb, kb, vb = (col_base + n for n in range(3))

    pairs = [(i, j) for i in range(nq) for j in range(i + 1)]
    qi_tab = jnp.asarray([i for i, _ in pairs], jnp.int32)
    kj_tab = jnp.asarray([j for _, j in pairs], jnp.int32)
    grid_spec = pltpu.PrefetchScalarGridSpec(
        num_scalar_prefetch=2,
        grid=(batch, len(pairs)),
        in_specs=[
            pl.BlockSpec((tq, width), lambda b, s, qi, kj: (off + b * nq + qi[s], qb)),
            pl.BlockSpec((tq, width), lambda b, s, qi, kj: (off + b * nq + kj[s], kb)),
            pl.BlockSpec((tq, width), lambda b, s, qi, kj: (off + b * nq + kj[s], vb)),
            pl.BlockSpec((1, H_FOX, tq), lambda b, s, qi, kj: (b, 0, qi[s])),
            pl.BlockSpec((1, tq, H_FOX), lambda b, s, qi, kj: (b, kj[s], 0)),
        ],
        out_specs=pl.BlockSpec((tq, width), lambda b, s, qi, kj: (off + b * nq + qi[s], 0)),
        scratch_shapes=[pltpu.VMEM((H_FOX, 1, tq), F32), pltpu.VMEM((H_FOX, 1, tq), F32),
                        pltpu.VMEM((H_FOX, DH_FOX, tq), F32)],
    )
    return pl.pallas_call(
        functools.partial(_fox_prompt_kernel, tq=tq, scale=DH_FOX ** -0.5),
        out_shape=jax.ShapeDtypeStruct((proj.shape[0], width), BF16),
        grid_spec=grid_spec,
        compiler_params=_params(("parallel", "arbitrary"), 48),
        name="fox_prompt",
    )(qi_tab, kj_tab, proj, proj, proj, c_row, c_col)


def _fox_sample_kernel(q_ref, kc_ref, vc_ref, kn_ref, vn_ref, cq_ref, ckc_ref, ckn_ref, o_ref,
                       m_ref, l_ref, acc_ref, s_ref, p_ref, *, tq, tkc, scale):
    j = pl.program_id(1)
    last = pl.num_programs(1) - 1

    @pl.when(j == 0)
    def _():
        m_ref[...] = jnp.full(m_ref.shape, NEG_INF, F32)
        l_ref[...] = jnp.zeros(l_ref.shape, F32)
        acc_ref[...] = jnp.zeros(acc_ref.shape, F32)

    cq = jnp.concatenate([cq_ref[0, :, h:h + 1] for h in range(H_FOX)], axis=0) * LOG2E

    def scores(h, k, ck, visible=None):
        q = (q_ref[:, h * DH_FOX:(h + 1) * DH_FOX] * (scale * LOG2E)).astype(BF16)
        s = lax.dot_general(q, k, _NT, preferred_element_type=F32) - ck[h:h + 1, :]
        if visible is not None:
            s = jnp.where(visible, s, NEG_INF)
        s_ref[h * tq:(h + 1) * tq, :s.shape[1]] = s

    def softmax_update(width):
        s = s_ref[:, :width]
        m_prev = m_ref[...]
        m_new = jnp.maximum(m_prev, jnp.max(s, axis=-1, keepdims=True) + cq)
        p = jnp.exp2(s - (m_new - cq))
        alpha = jnp.exp2(m_prev - m_new)
        l_ref[...] = alpha * l_ref[...] + jnp.sum(p, axis=-1, keepdims=True)
        m_ref[...] = m_new
        p_ref[:, :width] = p.astype(BF16)
        return alpha

    def values(h, v, alpha, width):
        rows = slice(h * tq, (h + 1) * tq)
        acc_ref[h] = alpha[rows] * acc_ref[h] + jnp.dot(p_ref[rows, :width], v, preferred_element_type=F32)

    ck = ckc_ref[0] * LOG2E
    for h in range(H_FOX):
        scores(h, kc_ref[pl.ds(h, tkc, stride=H_FOX), :].astype(BF16), ck)
    alpha = softmax_update(tkc)
    for h in range(H_FOX):
        values(h, vc_ref[pl.ds(h, tkc, stride=H_FOX), :].astype(BF16), alpha, tkc)

    @pl.when(j == last)
    def _():
        row = lax.broadcasted_iota(jnp.int32, (tq, tq), 0)
        col = lax.broadcasted_iota(jnp.int32, (tq, tq), 1)
        visible = col <= row
        ck = ckn_ref[0] * LOG2E
        for h in range(H_FOX):
            scores(h, kn_ref[:, h * DH_FOX:(h + 1) * DH_FOX].astype(BF16), ck, visible)
        alpha = softmax_update(tq)
        for h in range(H_FOX):
            cols = slice(h * DH_FOX, (h + 1) * DH_FOX)
            values(h, vn_ref[:, cols].astype(BF16), alpha, tq)
            o_ref[:, cols] = (acc_ref[h] / l_ref[h * tq:(h + 1) * tq]).astype(o_ref.dtype)


def _fox_sample(proj, cache_k, cache_v, c_col, c_row_cache, c_row_new, dest, *, layer, batch, seq,
                row_off, tkc, col_base):
    past = cache_k.shape[2] // H_FOX
    nkc = past // tkc
    off = row_off // seq
    width = H_FOX * DH_FOX
    qb, kb, vb = (col_base + n for n in range(3))
    cache_spec = pl.BlockSpec((None, None, tkc * H_FOX, DH_FOX), lambda b, j: (layer, b, j, 0))

    call = _call_into(
        functools.partial(_fox_sample_kernel, tq=seq, tkc=tkc, scale=DH_FOX ** -0.5), {0: dest}, 8,
        out_shape=jax.ShapeDtypeStruct((proj.shape[0], width), BF16),
        grid=(batch, nkc),
        in_specs=[
            pl.BlockSpec((seq, width), lambda b, j: (off + b, qb)),
            cache_spec,
            cache_spec,
            pl.BlockSpec((seq, width), lambda b, j: (off + b, kb)),
            pl.BlockSpec((seq, width), lambda b, j: (off + b, vb)),
            pl.BlockSpec((1, seq, H_FOX), lambda b, j: (b, 0, 0)),
            pl.BlockSpec((1, H_FOX, tkc), lambda b, j: (b, 0, j)),
            pl.BlockSpec((1, H_FOX, seq), lambda b, j: (b, 0, 0)),
        ],
        out_specs=pl.BlockSpec((seq, width), lambda b, j: (off + b, 0)),
        scratch_shapes=[pltpu.VMEM((H_FOX * seq, 1), F32), pltpu.VMEM((H_FOX * seq, 1), F32),
                        pltpu.VMEM((H_FOX, seq, DH_FOX), F32),
                        pltpu.VMEM((H_FOX * seq, tkc), F32), pltpu.VMEM((H_FOX * seq, tkc), BF16)],
        compiler_params=_params(("parallel", "arbitrary"), 48),
        name="fox_sample",
    )
    return call(proj, cache_k, cache_v, proj, proj, c_col, c_row_cache, c_row_new)


def _out_proj_kernel(a_ref, b_ref, w_ref, xp_ref, xs_ref, o_ref, *, ntp):
    ka = a_ref.shape[1]
    o_ref[...] = (jnp.where(pl.program_id(0) < ntp, xp_ref[...], xs_ref[...])
                  + jnp.dot(a_ref[...], w_ref[:ka, :], preferred_element_type=F32)
                  + jnp.dot(b_ref[...], w_ref[ka:, :], preferred_element_type=F32))


def _out_proj(a, b, w, x_src, *, layer, n_p, tm):
    xp, xs, xs_off = x_src
    t, d = a.shape[0], xp.shape[1]
    ka, kb = a.shape[1], b.shape[1]
    ntp = n_p // tm
    return pl.pallas_call(
        functools.partial(_out_proj_kernel, ntp=ntp),
        out_shape=jax.ShapeDtypeStruct((t, d), F32),
        grid=(t // tm,),
        in_specs=[
            pl.BlockSpec((tm, ka), lambda i: (i, 0)),
            pl.BlockSpec((tm, kb), lambda i: (i, 0)),
            pl.BlockSpec((None, ka + kb, d), lambda i: (layer, 0, 0)),
            *_two_group_rows((tm, d), ntp, (t - n_p) // tm, xs_off // tm),
        ],
        out_specs=pl.BlockSpec((tm, d), lambda i: (i, 0)),
        compiler_params=_params(("parallel",), 56),
        name="out_proj",
    )(a, b, w, xp, xs)


def _pool_kernel(x_ref, halo_ref, buf_ref, g_ref, w_ref, ps_ref, o_ref, bo_ref, full_ref, sum_a_ref,
                 sum_b_ref, *, tm, pos0):
    i = pl.program_id(1)
    g = g_ref[...]
    x = x_ref[...]
    u = _rms(x, g)
    hist = jnp.where(i == 0, buf_ref[0], _rms(halo_ref[...], g))
    lo, top = POOL_PAD, POOL_PAD + POOL_HALO
    end = top + tm
    for ref in (full_ref, sum_a_ref, sum_b_ref):
        ref[0:lo, :] = jnp.zeros((lo, x.shape[1]), F32)
    full_ref[lo:top, :] = hist
    full_ref[top:end, :] = u
    pos = pos0 + i * tm + lax.broadcasted_iota(jnp.int32, (tm, 1), 0)
    gc = x.shape[1] // len(POOL_WINDOWS)
    src, width = full_ref, 1
    for n, win in enumerate(POOL_WINDOWS):
        cols = slice(n * gc, (n + 1) * gc)
        if n == len(POOL_WINDOWS) - 1:
            total = src[top:end, cols] + src[top - width:end - width, cols]
        else:
            dst = (sum_a_ref, sum_b_ref)[n % 2]
            dst[lo:end, n * gc:] = src[lo:end, n * gc:] + src[lo - width:end - width, n * gc:]
            total = dst[top:end, cols]
            src = dst
        width = win
        inv_cnt = 1.0 / jnp.minimum(win, pos + 1).astype(F32)
        diff = (total * inv_cnt - u[:, cols]).astype(BF16)
        y = jnp.dot(diff, w_ref[n], preferred_element_type=F32) * ps_ref[:, cols]
        o_ref[:, cols] = x[:, cols] + y
    bo_ref[0] = u[tm - POOL_HALO:, :]


def _pool_mixer(x, buf, g, w, ps, dest, *, layer, mixer, batch, seq, row_off, tm, pos0):
    d = x.shape[1]
    nt = seq // tm
    off = row_off // tm
    per_halo = tm // POOL_HALO
    halo_off = row_off // POOL_HALO

    def halo_map(b, i):
        return (jnp.maximum(halo_off + (b * nt + i) * per_halo - 1, 0), 0)

    rows = pl.BlockSpec((tm, d), lambda b, i: (off + b * nt + i, 0))
    call = _call_into(
        functools.partial(_pool_kernel, tm=tm, pos0=pos0), {0: dest}, 6,
        out_shape=(jax.ShapeDtypeStruct(x.shape, F32),
                   jax.ShapeDtypeStruct((batch, POOL_HALO, d), F32)),
        grid=(batch, nt),
        in_specs=[
            rows,
            pl.BlockSpec((POOL_HALO, d), halo_map),
            pl.BlockSpec((1, POOL_HALO, d), lambda b, i: (b, 0, 0)),
            _layer_vec(layer, d),
            pl.BlockSpec((None,) + w.shape[1:], lambda b, i: (mixer, 0, 0, 0)),
            _layer_vec(mixer, d),
        ],
        out_specs=(rows, pl.BlockSpec((1, POOL_HALO, d), lambda b, i: (b, 0, 0))),
        scratch_shapes=[pltpu.VMEM((POOL_PAD + POOL_HALO + tm, d), F32)] * 3,
        compiler_params=_params(("parallel", "arbitrary"), 48),
        name="pool_mixer",
    )
    return call(x, x, buf, g, w, ps)


def _ffn_kernel(x_ref, g_ref, wg_ref, wu_ref, wd_ref, o_ref, h_ref):
    @pl.when(pl.program_id(1) == 0)
    def _():
        x = x_ref[...]
        h_ref[...] = _rms(x, g_ref[...]).astype(h_ref.dtype)
        o_ref[...] = x

    h = h_ref[...]
    a = jnp.dot(h, wg_ref[...].astype(BF16), preferred_element_type=F32)
    u = jnp.dot(h, wu_ref[...].astype(BF16), preferred_element_type=F32)
    act = (a * _sigmoid(a) * u).astype(BF16)
    o_ref[...] += jnp.dot(act, wd_ref[...].astype(BF16), preferred_element_type=F32)


def _ffn(x, g, wg, wu, wd, *, layer, tm, tf):
    t, d = x.shape
    f = wg.shape[2]
    return pl.pallas_call(
        _ffn_kernel,
        out_shape=jax.ShapeDtypeStruct((t, d), F32),
        grid=(t // tm, f // tf),
        in_specs=[
            pl.BlockSpec((tm, d), lambda i, j: (i, 0)),
            _layer_vec(layer, d),
            pl.BlockSpec((None, d, tf), lambda i, j: (layer, 0, j)),
            pl.BlockSpec((None, d, tf), lambda i, j: (layer, 0, j)),
            pl.BlockSpec((None, tf, d), lambda i, j: (layer, j, 0)),
        ],
        out_specs=pl.BlockSpec((tm, d), lambda i, j: (i, 0)),
        scratch_shapes=[pltpu.VMEM((tm, d), BF16)],
        compiler_params=_params(("parallel", "arbitrary"), 56),
        name="swiglu_ffn",
    )(x, g, wg, wu, wd)


def _ple_update(x_ref, g_ref, wg_ref, pp_ref, ps_ref, wp_ref, ntp):
    x = x_ref[...]
    h = _rms(x, g_ref[...]).astype(BF16)
    gate = _sigmoid(jnp.dot(h, wg_ref[...], preferred_element_type=F32))
    p = jnp.where(pl.program_id(0) < ntp, pp_ref[...], ps_ref[...])
    emb = jnp.dot(p.astype(BF16), wp_ref[...], preferred_element_type=F32)
    return x + gate * emb


def _ple_kernel(x_ref, g_ref, wg_ref, pp_ref, ps_ref, wp_ref, o_ref, *, ntp):
    o_ref[...] = _ple_update(x_ref, g_ref, wg_ref, pp_ref, ps_ref, wp_ref, ntp)


def _ple_final_kernel(x_ref, g_ref, wg_ref, pp_ref, ps_ref, wp_ref, gf_ref, yp_ref, ys_ref, *, ntp):
    y = _rms(_ple_update(x_ref, g_ref, wg_ref, pp_ref, ps_ref, wp_ref, ntp), gf_ref[...])

    @pl.when(pl.program_id(0) < ntp)
    def _():
        yp_ref[...] = y

    @pl.when(pl.program_id(0) >= ntp)
    def _():
        ys_ref[...] = y


def _ple(x, g, wg, p_prompt, p_sample, wp, final_g, *, layer, n_p, tm):
    t, d = x.shape
    pd = p_prompt.shape[2]
    ntp, nts = n_p // tm, (t - n_p) // tm
    rows = pl.BlockSpec((tm, d), lambda i: (i, 0))
    in_specs = [
        rows,
        _layer_vec(layer, d),
        pl.BlockSpec((None, d, d), lambda i: (layer, 0, 0)),
        pl.BlockSpec((None, tm, pd), lambda i: (layer, jnp.minimum(i, ntp - 1), 0)),
        pl.BlockSpec((None, tm, pd), lambda i: (layer, jnp.clip(i - ntp, 0, nts - 1), 0)),
        pl.BlockSpec((None, pd, d), lambda i: (layer, 0, 0)),
    ]
    if final_g is None:
        return pl.pallas_call(
            functools.partial(_ple_kernel, ntp=ntp),
            out_shape=jax.ShapeDtypeStruct((t, d), F32),
            grid=(t // tm,),
            in_specs=in_specs,
            out_specs=rows,
            compiler_params=_params(("parallel",), 56),
            name="gated_embedding",
        )(x, g, wg, p_prompt, p_sample, wp)
    return pl.pallas_call(
        functools.partial(_ple_final_kernel, ntp=ntp),
        out_shape=(jax.ShapeDtypeStruct((n_p, d), F32), jax.ShapeDtypeStruct((t - n_p, d), F32)),
        grid=(t // tm,),
        in_specs=in_specs + [pl.BlockSpec((1, d), lambda i: (0, 0))],
        out_specs=(pl.BlockSpec((tm, d), lambda i: (jnp.minimum(i, ntp - 1), 0)),
                   pl.BlockSpec((tm, d), lambda i: (jnp.clip(i - ntp, 0, nts - 1), 0))),
        compiler_params=_params(("arbitrary",), 56),
        name="gated_embedding_final_norm",
    )(x, g, wg, p_prompt, p_sample, wp, final_g)


def _rope_tables(pos0, seq):
    half = DK_RET // 2
    inv = ROPE_BASE ** (-jnp.arange(half, dtype=F32) / half)
    ang = (pos0 + jnp.arange(seq, dtype=jnp.int32)).astype(F32)[:, None] * inv[None, :]
    return jnp.cos(ang), jnp.sin(ang)


def _decay_tables():
    log_g = jnp.log1p(-jnp.power(2.0, -5.0 - jnp.arange(H_RET, dtype=F32)))
    idx = jnp.arange(CHUNK, dtype=F32)
    intra = jnp.exp(log_g[:, None, None] * jnp.abs(idx[:, None] - idx[None, :]))
    q_dec = jnp.exp(log_g[:, None] * (idx[None, :] + 1.0))[..., None]
    k_dec = jnp.exp(log_g[:, None] * (CHUNK - 1.0 - idx[None, :]))[..., None]
    s_dec = jnp.exp(log_g * CHUNK)[:, None, None]
    return intra, q_dec, k_dec, s_dec


def _tile(n, want):
    t = min(n, want)
    while n % t:
        t //= 2
    return t


def kernel(x_prompt, x_sample, p_prompt, p_sample, state_ret, cache_fox_k, cache_fox_v, cache_fox_logf,
           state_pool, norm_mix, w_in, b_forget, w_out, w_pool, pool_scale, norm_ffn, w_gate, w_up, w_down,
           norm_ple, ple_gate, ple_proj, final_norm):
    bp, tp, d = x_prompt.shape
    bs, ts, _ = x_sample.shape
    depth = norm_mix.shape[0]
    past = cache_fox_k.shape[2]
    n_p, n_s = bp * tp, bs * ts
    n_tok = n_p + n_s
    ret_w = H_RET * DK_RET
    fox_w = H_FOX * DH_FOX
    main_w = 4 * ret_w + 3 * fox_w
    fox_base = (4 * ret_w) // fox_w

    x = None
    p_prompt = p_prompt.reshape(depth, n_p, -1)
    p_sample = p_sample.reshape(depth, n_s, -1)

    as_rows = lambda a: a.reshape(a.shape[0], 1, a.shape[1])
    norm_mix, norm_ffn, norm_ple, pool_scale = map(as_rows, (norm_mix, norm_ffn, norm_ple, pool_scale))
    w_in_b, w_out_b, w_pool_b = w_in.astype(BF16), w_out.astype(BF16), w_pool.astype(BF16)
    ple_gate_b, ple_proj_b = ple_gate.astype(BF16), ple_proj.astype(BF16)
    w_f = jnp.pad(w_in[:, :, main_w:], ((0, 0), (0, 0), (0, LANES - H_FOX))).astype(BF16)
    b_f = jnp.pad(b_forget.astype(F32), ((0, 0), (0, LANES - H_FOX)))[:, None, :]
    cache_k = cache_fox_k.reshape(cache_fox_k.shape[:2] + (past * H_FOX, DH_FOX))
    cache_v = cache_fox_v.reshape(cache_fox_v.shape[:2] + (past * H_FOX, DH_FOX))

    decay = _decay_tables()
    rope_p = _rope_tables(0, tp)
    rope_s = _rope_tables(past, ts)
    zero_state = jnp.zeros((1, bp) + state_ret.shape[2:], F32)
    zero_buf = jnp.zeros((bp, POOL_HALO, d), F32)

    n_even = w_in.shape[0]
    kv_new = (None,) * 4
    ret_p = ret_s = None
    new_lf, new_pool_p, new_pool_s = [], [], []
    for i in range(depth):
        if i % 2 == 0:
            e = i // 2
            x_src = (x_prompt.reshape(n_p, d), x_sample.reshape(n_s, d), 0) if x is None else (x, x, n_p)
            proj, lf, *kv_new = _in_proj(x_src, norm_mix, w_in_b, w_f, b_f, kv_new, layer=i, wlayer=e,
                                         n_layers=n_even, n=main_w, n_p=n_p, n_s=n_s,
                                         k_col=4 * ret_w + fox_w, tm=_tile(math.gcd(n_p, n_s), 512))
            lf = lf[:, :H_FOX]
            lf_p = lf[:n_p].reshape(bp, tp, H_FOX)
            lf_s = lf[n_p:].reshape(bs, ts, H_FOX)
            c_row_p = _cumsum_rows(jnp.transpose(lf_p, (0, 2, 1)))
            lf_all_s = jnp.concatenate([cache_fox_logf[e].astype(F32), lf_s], axis=1)
            c_row_s = _cumsum_rows(jnp.transpose(lf_all_s, (0, 2, 1)))
            c_col_p = jnp.transpose(c_row_p, (0, 2, 1))
            c_col_s = jnp.transpose(c_row_s[:, :, past:], (0, 2, 1))

            o_r, ret_p = _retention(proj, *rope_p, decay, zero_state, None, ret_p, layer=0, out_layer=e,
                                    n_layers=n_even, batch=bp, seq=tp, row_off=0,
                                    chunks=_tile(tp, 512) // CHUNK, heads=2)
            o_r, ret_s = _retention(proj, *rope_s, decay, state_ret.astype(F32), o_r, ret_s, layer=e,
                                    out_layer=e, n_layers=n_even, batch=bs, seq=ts, row_off=n_p,
                                    chunks=ts // CHUNK, heads=H_RET)
            o_f = _fox_prompt(proj, c_col_p, c_row_p, batch=bp, seq=tp, row_off=0, tq=_tile(tp, 512),
                              col_base=fox_base)
            o_f = _fox_sample(proj, cache_k, cache_v, c_col_s, c_row_s, c_row_s[:, :, past:], o_f,
                              layer=e, batch=bs, seq=ts, row_off=n_p, tkc=_tile(past, 2048),
                              col_base=fox_base)
            x = _out_proj(o_r, o_f, w_out_b, x_src, layer=e, n_p=n_p, tm=_tile(math.gcd(n_p, n_s), 512))

            new_lf.append((lf_p, lf_s))
        else:
            o = i // 2
            buf_s = jnp.pad(state_pool[o].astype(F32), ((0, 0), (1, 0), (0, 0)))
            x_new, pool_p = _pool_mixer(x, zero_buf, norm_mix, w_pool_b, pool_scale, None, layer=i, mixer=o,
                                        batch=bp, seq=tp, row_off=0, tm=_tile(tp, 512), pos0=0)
            x, pool_s = _pool_mixer(x, buf_s, norm_mix, w_pool_b, pool_scale, x_new, layer=i, mixer=o,
                                    batch=bs, seq=ts, row_off=n_p, tm=ts, pos0=past)
            new_pool_p.append(pool_p[:, 1:])
            new_pool_s.append(pool_s[:, 1:])
        x = _ffn(x, norm_ffn, w_gate, w_up, w_down, layer=i, tm=_tile(n_tok, 1024), tf=256)
        x = _ple(x, norm_ple, ple_gate_b, p_prompt, p_sample, ple_proj_b,
                 final_norm[None, :] if i == depth - 1 else None, layer=i, n_p=n_p,
                 tm=_tile(math.gcd(n_p, n_s), 512))
    y_p, y_s = x
    k_p, k_s, v_p, v_s = kv_new
    prompt_kv = (n_even, bp, tp, H_FOX, DH_FOX)
    sample_kv = (n_even, bs, ts, H_FOX, DH_FOX)
    return (
        y_p.reshape(bp, tp, d),
        y_s.reshape(bs, ts, d),
        ret_p,
        ret_s.astype(state_ret.dtype),
        k_p.reshape(prompt_kv),
        k_s.reshape(sample_kv),
        v_p.reshape(prompt_kv),
        v_s.reshape(sample_kv),
        jnp.stack([lf[0] for lf in new_lf]),
        jnp.stack([lf[1] for lf in new_lf]),
        jnp.stack(new_pool_p),
        jnp.stack(new_pool_s),
    )
```
